```python
import math
import jax, jax.numpy as jnp
from jax import lax
import numpy as np

D_MODEL = 4096
BATCH = 2
SEQ = 4096
DEPTH = 2

CHUNK = 64
HEAD_DIM = 128
N_HEADS = D_MODEL // HEAD_DIM
MIX = N_HEADS * HEAD_DIM
GDN_CONV = 4
FFN_CONV = 3
FFN_DIM = ((8 * D_MODEL // 3 + 255) // 256) * 256
Q_BLOCK = 128
N_MIXERS = 2
N_GDN = (DEPTH + 1) // 2
N_FOX = DEPTH // 2
GDN_IN = 4 * MIX + 2 * N_HEADS
FOX_IN = 4 * MIX + N_HEADS
EPS = 1e-6

kernel_name = 'chunk_causal_gdn_fox_hybrid'


def rms_norm(x, g):
    xf = x.astype(jnp.float32)
    y = xf * lax.rsqrt(jnp.mean(xf * xf, axis=-1, keepdims=True) + EPS)
    return (y * g.astype(jnp.float32)).astype(x.dtype)


def l2_norm(x):
    return x * lax.rsqrt(jnp.sum(x * x, axis=-1, keepdims=True) + EPS)


def causal_dwconv(x, w):
    k_w = w.shape[0]
    s = x.shape[1]
    xp = jnp.pad(x, ((0, 0), (k_w - 1, 0), (0, 0)))
    return sum(xp[:, j:j + s] * w[j] for j in range(k_w))


def gated_delta_rule(q, k, v, beta, log_a):
    b, h, s, dk = q.shape
    dv = v.shape[-1]
    n = s // CHUNK
    q = q.reshape(b, h, n, CHUNK, dk)
    k = k.reshape(b, h, n, CHUNK, dk)
    v = v.reshape(b, h, n, CHUNK, dv)
    beta = beta.reshape(b, h, n, CHUNK)
    g = jnp.cumsum(log_a.reshape(b, h, n, CHUNK), axis=-1)
    incl = jnp.tril(jnp.ones((CHUNK, CHUNK), dtype=bool))
    strict = jnp.tril(jnp.ones((CHUNK, CHUNK), dtype=bool), -1)
    decay = jnp.exp(jnp.where(incl, g[..., :, None] - g[..., None, :], -jnp.inf))
    kb = k * beta[..., None]
    vb = v * beta[..., None]
    m = jnp.where(strict, jnp.einsum('bhncd,bhnsd->bhncs', kb, k) * decay, 0.0)
    eye = jnp.eye(CHUNK, dtype=jnp.float32)
    t_mat = lax.linalg.triangular_solve(eye + m, jnp.broadcast_to(eye, m.shape),
                                        left_side=True, lower=True, unit_diagonal=True)
    u = jnp.einsum('bhncs,bhnsv->bhncv', t_mat, vb)
    w = jnp.einsum('bhncs,bhnsd->bhncd', t_mat, kb * jnp.exp(g)[..., None])
    attn = jnp.where(incl, jnp.einsum('bhncd,bhnsd->bhncs', q, k) * decay, 0.0)
    g_last = g[..., -1]
    q_head = q * jnp.exp(g)[..., None]
    k_tail = k * jnp.exp(g_last[..., None] - g)[..., None]

    def step(state, inp):
        u_i, w_i, qh_i, kt_i, a_i, gl_i = inp
        v_new = u_i - jnp.einsum('bhck,bhkv->bhcv', w_i, state)
        o_i = jnp.einsum('bhck,bhkv->bhcv', qh_i, state) + jnp.einsum('bhcs,bhsv->bhcv', a_i, v_new)
        state = state * jnp.exp(gl_i)[..., None, None] + jnp.einsum('bhck,bhcv->bhkv', kt_i, v_new)
        return state, o_i

    front = lambda t: jnp.moveaxis(t, 2, 0)
    state0 = jnp.zeros((b, h, dk, dv), jnp.float32)
    _, o = lax.scan(step, state0, (front(u), front(w), front(q_head), front(k_tail), front(attn), front(g_last)))
    return jnp.moveaxis(o, 0, 2).reshape(b, h, s, dv)


def gdn_mixer(hx, w_in, conv_w, a_log_param, dt_bias, norm_g, w_out):
    b, s, _ = hx.shape
    proj = hx @ w_in
    qkv = jax.nn.silu(causal_dwconv(proj[..., :3 * MIX], conv_w))
    z = proj[..., 3 * MIX:4 * MIX]
    b_raw = proj[..., 4 * MIX:4 * MIX + N_HEADS].astype(jnp.float32)
    a_raw = proj[..., 4 * MIX + N_HEADS:].astype(jnp.float32)
    heads = lambda t: t.astype(jnp.float32).reshape(b, s, N_HEADS, HEAD_DIM).transpose(0, 2, 1, 3)
    q = l2_norm(heads(qkv[..., :MIX])) * (HEAD_DIM ** -0.5)
    k = l2_norm(heads(qkv[..., MIX:2 * MIX]))
    v = heads(qkv[..., 2 * MIX:])
    beta = jax.nn.sigmoid(b_raw).transpose(0, 2, 1)
    log_a = (-jnp.exp(a_log_param.astype(jnp.float32))
             * jax.nn.softplus(a_raw + dt_bias.astype(jnp.float32))).transpose(0, 2, 1)
    o = gated_delta_rule(q, k, v, beta, log_a).transpose(0, 2, 1, 3)
    zg = jax.nn.silu(z.astype(jnp.float32).reshape(b, s, N_HEADS, HEAD_DIM))
    o = (rms_norm(o, norm_g) * zg).astype(hx.dtype).reshape(b, s, MIX)
    return o @ w_out


def fox_mixer(hx, w_in, b_f, q_norm_g, k_norm_g, w_out):
    b, s, _ = hx.shape
    proj = hx @ w_in
    heads = lambda t: t.reshape(b, s, N_HEADS, HEAD_DIM)
    q = rms_norm(heads(proj[..., :MIX]), q_norm_g).transpose(0, 2, 1, 3)
    k = rms_norm(heads(proj[..., MIX:2 * MIX]), k_norm_g).transpose(0, 2, 1, 3)
    v = heads(proj[..., 2 * MIX:3 * MIX]).transpose(0, 2, 1, 3)
    o_gate = proj[..., 3 * MIX:4 * MIX]
    log_f = jax.nn.log_sigmoid((proj[..., 4 * MIX:] + b_f).astype(jnp.float32))
    cum = jnp.cumsum(log_f, axis=1).transpose(0, 2, 1)
    nb = s // Q_BLOCK
    q_blocks = jnp.moveaxis(q.reshape(b, N_HEADS, nb, Q_BLOCK, HEAD_DIM), 2, 0)
    c_blocks = jnp.moveaxis(cum.reshape(b, N_HEADS, nb, Q_BLOCK), 2, 0)
    pos_k = jnp.arange(s)
    scale = HEAD_DIM ** -0.5

    def block(args):
        q_i, cq_i, i = args
        logits = (jnp.einsum('bhqd,bhkd->bhqk', q_i, k).astype(jnp.float32) * scale
                  + cq_i[..., :, None] - cum[:, :, None, :])
        pos_q = i * Q_BLOCK + jnp.arange(Q_BLOCK)
        logits = jnp.where(pos_k[None, :] <= pos_q[:, None], logits, -jnp.inf)
        p = jax.nn.softmax(logits, axis=-1)
        return jnp.einsum('bhqk,bhkd->bhqd', p.astype(v.dtype), v)

    o = lax.map(block, (q_blocks, c_blocks, jnp.arange(nb)))
    o = jnp.moveaxis(o, 0, 2).reshape(b, N_HEADS, s, HEAD_DIM).transpose(0, 2, 1, 3).reshape(b, s, MIX)
    return (o * jax.nn.sigmoid(o_gate)) @ w_out


def conv_ffn(hx, w_up, conv_w, w_down):
    u = causal_dwconv(hx @ w_up, conv_w)
    gate, val = jnp.split(u, 2, axis=-1)
    return (jax.nn.silu(gate) * val) @ w_down


def setup_inputs(seed: int = 0) -> dict:
    key = jax.random.key(seed)
    ks = jax.random.split(key, 24)
    d = D_MODEL
    nrm = lambda k, shape, sd: jax.random.normal(k, shape, jnp.float32) * sd
    x = nrm(ks[0], (BATCH, SEQ, d), 1.0)
    c = nrm(ks[1], (BATCH, d), 1.0)
    ada_w = nrm(ks[2], (DEPTH, d, 6 * d), 0.5 * d ** -0.5)
    ada_b = nrm(ks[3], (DEPTH, 6 * d), 0.02)
    norm_mix_g = 1.0 + nrm(ks[4], (DEPTH, d), 0.05)
    norm_ffn_g = 1.0 + nrm(ks[5], (DEPTH, d), 0.05)
    gdn_w_in = nrm(ks[6], (N_GDN, d, GDN_IN), d ** -0.5)
    gdn_conv_w = nrm(ks[7], (N_GDN, GDN_CONV, 3 * MIX), GDN_CONV ** -0.5)
    gdn_A_log = jnp.log(jax.random.uniform(ks[8], (N_GDN, N_HEADS), jnp.float32, 1.0, 16.0))
    dt = jnp.exp(jax.random.uniform(ks[9], (N_GDN, N_HEADS), jnp.float32, math.log(1e-3), math.log(1e-1)))
    gdn_dt_bias = dt + jnp.log(-jnp.expm1(-dt))
    gdn_norm_g = 1.0 + nrm(ks[10], (N_GDN, HEAD_DIM), 0.05)
    gdn_w_out = nrm(ks[11], (N_GDN, MIX, d), MIX ** -0.5)
    fox_w_in = nrm(ks[12], (N_FOX, d, FOX_IN), d ** -0.5)
    fox_b_f = jax.random.uniform(ks[13], (N_FOX, N_HEADS), jnp.float32, 1.0, 4.0)
    fox_q_norm_g = 1.0 + nrm(ks[14], (N_FOX, HEAD_DIM), 0.05)
    fox_k_norm_g = 1.0 + nrm(ks[15], (N_FOX, HEAD_DIM), 0.05)
    fox_w_out = nrm(ks[16], (N_FOX, MIX, d), MIX ** -0.5)
    ffn_w_up = nrm(ks[17], (DEPTH, d, 2 * FFN_DIM), d ** -0.5)
    ffn_conv_w = nrm(ks[18], (DEPTH, FFN_CONV, 2 * FFN_DIM), FFN_CONV ** -0.5)
    ffn_w_down = nrm(ks[19], (DEPTH, FFN_DIM, d), FFN_DIM ** -0.5)
    final_norm_g = 1.0 + nrm(ks[20], (d,), 0.05)
    return {'x': x, 'c': c, 'ada_w': ada_w, 'ada_b': ada_b,
            'norm_mix_g': norm_mix_g, 'norm_ffn_g': norm_ffn_g,
            'gdn_w_in': gdn_w_in, 'gdn_conv_w': gdn_conv_w, 'gdn_A_log': gdn_A_log,
            'gdn_dt_bias': gdn_dt_bias, 'gdn_norm_g': gdn_norm_g, 'gdn_w_out': gdn_w_out,
            'fox_w_in': fox_w_in, 'fox_b_f': fox_b_f, 'fox_q_norm_g': fox_q_norm_g,
            'fox_k_norm_g': fox_k_norm_g, 'fox_w_out': fox_w_out,
            'ffn_w_up': ffn_w_up, 'ffn_conv_w': ffn_conv_w, 'ffn_w_down': ffn_w_down,
            'final_norm_g': final_norm_g}


def reference(x, c, ada_w, ada_b, norm_mix_g, norm_ffn_g,
              gdn_w_in, gdn_conv_w, gdn_A_log, gdn_dt_bias, gdn_norm_g, gdn_w_out,
              fox_w_in, fox_b_f, fox_q_norm_g, fox_k_norm_g, fox_w_out,
              ffn_w_up, ffn_conv_w, ffn_w_down, final_norm_g):
    c_act = jax.nn.silu(c)
    for i in range(DEPTH):
        j = i // N_MIXERS
        mod = (c_act @ ada_w[i] + ada_b[i])[:, None, :]
        sh1, sc1, g1, sh2, sc2, g2 = jnp.split(mod, 6, axis=-1)
        h = rms_norm(x, norm_mix_g[i]) * (1.0 + sc1) + sh1
        if i % N_MIXERS == 0:
            mixed = gdn_mixer(h, gdn_w_in[j], gdn_conv_w[j], gdn_A_log[j], gdn_dt_bias[j],
                              gdn_norm_g[j], gdn_w_out[j])
        else:
            mixed = fox_mixer(h, fox_w_in[j], fox_b_f[j], fox_q_norm_g[j], fox_k_norm_g[j], fox_w_out[j])
        x = x + g1 * mixed
        h = rms_norm(x, norm_ffn_g[i]) * (1.0 + sc2) + sh2
        x = x + g2 * conv_ffn(h, ffn_w_up[i], ffn_conv_w[i], ffn_w_down[i])
    return rms_norm(x, final_norm_g)
```

```python
import functools

import jax
import jax.numpy as jnp
from jax import lax
from jax.experimental import pallas as pl
from jax.experimental.pallas import tpu as pltpu

EPS = 1e-6
CHUNK = 64
HEAD_DIM = 128
LANES = 128
HALO = 16
VMEM_LIMIT_BYTES = 56 * 1024 * 1024

F32 = jnp.float32
BF16 = jnp.bfloat16
HIGHEST = lax.Precision.HIGHEST


def _params(*sem):
    return pltpu.CompilerParams(dimension_semantics=sem, vmem_limit_bytes=VMEM_LIMIT_BYTES)


def _sigmoid(x):
    return 1.0 / (1.0 + jnp.exp(-x))


def _silu(x):
    return x * _sigmoid(x)


def _softplus(x):
    return jnp.maximum(x, 0.0) + jnp.log(1.0 + jnp.exp(-jnp.abs(x)))


def _rms(x):
    return x * lax.rsqrt(jnp.mean(x * x, axis=-1, keepdims=True) + EPS)


def _dot(a, b):
    return jnp.dot(a, b, preferred_element_type=F32)


def _dot_nt(a, b):
    return lax.dot_general(a, b, (((1,), (1,)), ((), ())), preferred_element_type=F32)


def _dot_tn(a, b):
    return lax.dot_general(a, b, (((0,), (0,)), ((), ())), preferred_element_type=F32)


def _bdot(a, b):
    return lax.dot_general(a.astype(BF16), b.astype(BF16), (((2,), (1,)), ((0,), (0,))),
                           preferred_element_type=F32)


def _bdot_nt(a, b):
    return lax.dot_general(a, b, (((2,), (2,)), ((0,), (0,))), preferred_element_type=F32)


def _pick_lane(tile, lane):
    sel = (lax.broadcasted_iota(jnp.int32, (LANES, LANES), 0) == lane).astype(F32)
    return jnp.dot(tile, sel, precision=HIGHEST, preferred_element_type=F32)


def _prefix_sum_rows(x, span):
    row = lax.broadcasted_iota(jnp.int32, x.shape, 0) & (span - 1)
    s = 1
    while s < span:
        x = x + jnp.where(row >= s, pltpu.roll(x, s, 0), 0.0)
        s *= 2
    return x


def _ada_kernel(c_ref, w_ref, b_ref, o_ref):
    c = c_ref[...]
    o_ref[...] = _dot(_silu(c).astype(BF16), w_ref[...].astype(BF16)) + b_ref[...]


def _ada_mod(c, ada_w, ada_b, tn=1024):
    depth, d, n = ada_w.shape
    rows = 8
    c_pad = jnp.zeros((rows, d), F32).at[:c.shape[0]].set(c)
    return pl.pallas_call(
        _ada_kernel,
        grid=(depth, n // tn),
        in_specs=[pl.BlockSpec((rows, d), lambda l, j: (0, 0)),
                  pl.BlockSpec((None, d, tn), lambda l, j: (l, 0, j)),
                  pl.BlockSpec((None, 1, tn), lambda l, j: (l, 0, j))],
        out_specs=pl.BlockSpec((None, rows, tn), lambda l, j: (l, 0, j)),
        out_shape=jax.ShapeDtypeStruct((depth, rows, n), F32),
        compiler_params=_params("arbitrary", "arbitrary"),
        name="ada_mod",
    )(c_pad, ada_w, ada_b.reshape(depth, 1, n))


def _norm_mod_kernel(x_ref, g_ref, sc_ref, sh_ref, o_ref):
    y = _rms(x_ref[...]) * g_ref[...]
    o_ref[...] = (y * (1.0 + sc_ref[...]) + sh_ref[...]).astype(o_ref.dtype)


def _norm_mod(x, g, sc, sh, seq, tm=256):
    m, d = x.shape
    row_vec = pl.BlockSpec((None, 1, d), lambda i: ((i * tm) // seq, 0, 0))
    return pl.pallas_call(
        _norm_mod_kernel,
        grid=(m // tm,),
        in_specs=[pl.BlockSpec((tm, d), lambda i: (i, 0)),
                  pl.BlockSpec((1, d), lambda i: (0, 0)), row_vec, row_vec],
        out_specs=pl.BlockSpec((tm, d), lambda i: (i, 0)),
        out_shape=jax.ShapeDtypeStruct((m, d), BF16),
        compiler_params=_params("arbitrary"),
        name="norm_mod",
    )(x, g.reshape(1, d), sc, sh)


def _final_norm_kernel(x_ref, g_ref, o_ref):
    o_ref[...] = _rms(x_ref[...]) * g_ref[...]


def _final_norm(x, g, tm=256):
    m, d = x.shape
    return pl.pallas_call(
        _final_norm_kernel,
        grid=(m // tm,),
        in_specs=[pl.BlockSpec((tm, d), lambda i: (i, 0)),
                  pl.BlockSpec((1, d), lambda i: (0, 0))],
        out_specs=pl.BlockSpec((tm, d), lambda i: (i, 0)),
        out_shape=jax.ShapeDtypeStruct((m, d), F32),
        compiler_params=_params("arbitrary"),
        name="final_norm",
    )(x, g.reshape(1, d))


def _mm_kernel(a_ref, w_ref, o_ref):
    o_ref[...] = _dot(a_ref[...], w_ref[...]).astype(o_ref.dtype)


def _matmul(a, w, n_out, tm, tn, out_dtype=F32):
    m, k = a.shape
    tm, tn = min(tm, m), min(tn, n_out)
    return pl.pallas_call(
        _mm_kernel,
        grid=(m // tm, n_out // tn),
        in_specs=[pl.BlockSpec((tm, k), lambda i, j: (i, 0)),
                  pl.BlockSpec((k, tn), lambda i, j: (0, j))],
        out_specs=pl.BlockSpec((tm, tn), lambda i, j: (i, j)),
        out_shape=jax.ShapeDtypeStruct((m, n_out), out_dtype),
        compiler_params=_params("arbitrary", "arbitrary"),
        name="matmul",
    )(a, w)


def _mm_resid_kernel(a_ref, w_ref, x_ref, g_ref, o_ref):
    o_ref[...] = x_ref[...] + g_ref[...] * _dot(a_ref[...], w_ref[...])


def _matmul_resid(a, w, x, gate, seq, tm, tn):
    m, k = a.shape
    n = w.shape[1]
    tm, tn = min(tm, m), min(tn, n)
    return pl.pallas_call(
        _mm_resid_kernel,
        grid=(m // tm, n // tn),
        in_specs=[pl.BlockSpec((tm, k), lambda i, j: (i, 0)),
                  pl.BlockSpec((k, tn), lambda i, j: (0, j)),
                  pl.BlockSpec((tm, tn), lambda i, j: (i, j)),
                  pl.BlockSpec((None, 1, tn), lambda i, j: ((i * tm) // seq, 0, j))],
        out_specs=pl.BlockSpec((tm, tn), lambda i, j: (i, j)),
        out_shape=jax.ShapeDtypeStruct((m, n), F32),
        compiler_params=_params("arbitrary", "arbitrary"),
        name="matmul_resid",
    )(a, w, x, gate)


def _ffn_up_kernel(a_ref, ap_ref, wg_ref, wv_ref, cg_ref, cv_ref, o_ref, ae_ref, *, tm, seq):
    i = pl.program_id(0)
    j = pl.program_id(1)
    seq_start = (i * tm) % seq == 0

    @pl.when(j == 0)
    def _():
        ae_ref[HALO:, :] = a_ref[...]

    @pl.when(jnp.logical_and(j == 0, seq_start))
    def _():
        ae_ref[:HALO, :] = jnp.zeros((HALO, ae_ref.shape[1]), ae_ref.dtype)

    @pl.when(jnp.logical_and(j == 0, jnp.logical_not(seq_start)))
    def _():
        ae_ref[:HALO, :] = ap_ref[...]

    ae = ae_ref[...]

    def conv_branch(w_ref, c_ref):
        p = _dot(ae, w_ref[...])
        c = c_ref[...]
        y = c[0:1] * pltpu.roll(p, 2, 0) + c[1:2] * pltpu.roll(p, 1, 0) + c[2:3] * p
        return y[HALO:]

    gate = conv_branch(wg_ref, cg_ref)
    val = conv_branch(wv_ref, cv_ref)
    o_ref[...] = (_silu(gate) * val).astype(o_ref.dtype)


def _ffn_up(a, w_up, conv_w, seq, tm=1024, tn=256):
    m, k = a.shape
    f = w_up.shape[1] // 2
    tm = min(tm, seq)
    nj = f // tn
    return pl.pallas_call(
        functools.partial(_ffn_up_kernel, tm=tm, seq=seq),
        grid=(m // tm, nj),
        in_specs=[pl.BlockSpec((tm, k), lambda i, j: (i, 0)),
                  pl.BlockSpec((HALO, k), lambda i, j: (jnp.maximum(i * (tm // HALO) - 1, 0), 0)),
                  pl.BlockSpec((k, tn), lambda i, j: (0, j)),
                  pl.BlockSpec((k, tn), lambda i, j: (0, j + nj)),
                  pl.BlockSpec((conv_w.shape[0], tn), lambda i, j: (0, j)),
                  pl.BlockSpec((conv_w.shape[0], tn), lambda i, j: (0, j + nj))],
        out_specs=pl.BlockSpec((tm, tn), lambda i, j: (i, j)),
        out_shape=jax.ShapeDtypeStruct((m, f), BF16),
        scratch_shapes=[pltpu.VMEM((HALO + tm, k), BF16)],
        compiler_params=_params("arbitrary", "arbitrary"),
        name="ffn_up",
    )(a, a, w_up, w_up, conv_w, conv_w)


def _gdn_gates_kernel(a_ref, w_ref, alog_ref, dt_ref, beta_ref, g_ref):
    p = _dot(a_ref[...], w_ref[...])
    beta_ref[...] = _sigmoid(p[:, :LANES])
    log_a = -jnp.exp(alog_ref[...]) * _softplus(p[:, LANES:] + dt_ref[...])
    g_ref[...] = _prefix_sum_rows(log_a, CHUNK)


def _gdn_gates(a, w_small, a_log, dt_bias, tm=512):
    m, k = a.shape
    tm = min(tm, m)
    out = jax.ShapeDtypeStruct((m, LANES), F32)
    return pl.pallas_call(
        _gdn_gates_kernel,
        grid=(m // tm,),
        in_specs=[pl.BlockSpec((tm, k), lambda i: (i, 0)),
                  pl.BlockSpec((k, 2 * LANES), lambda i: (0, 0)),
                  pl.BlockSpec((1, LANES), lambda i: (0, 0)),
                  pl.BlockSpec((1, LANES), lambda i: (0, 0))],
        out_specs=[pl.BlockSpec((tm, LANES), lambda i: (i, 0))] * 2,
        out_shape=[out, out],
        compiler_params=_params("arbitrary"),
        name="gdn_gates",
    )(a, w_small, a_log, dt_bias)


def _gdn_kernel(q_ref, k_ref, v_ref, z_ref, beta_ref, g_ref, cq_ref, ck_ref, cv_ref, ng_ref,
                o_ref, state_ref, carry_ref, *, t_rows):
    h = pl.program_id(1)
    nc = t_rows // CHUNK

    @pl.when(pl.program_id(2) == 0)
    def _():
        state_ref[...] = jnp.zeros(state_ref.shape, F32)
        carry_ref[...] = jnp.zeros(carry_ref.shape, F32)

    def conv_silu(x_ref, c_ref, idx):
        x = x_ref[...]
        xe = jnp.concatenate([carry_ref[idx], x], axis=0)
        carry_ref[idx] = x[t_rows - 8:]
        c = c_ref[...]
        y = (c[0:1] * pltpu.roll(xe, 3, 0) + c[1:2] * pltpu.roll(xe, 2, 0)
             + c[2:3] * pltpu.roll(xe, 1, 0) + c[3:4] * xe)
        return _silu(y[8:])

    def l2n(x):
        return x * lax.rsqrt(jnp.sum(x * x, axis=-1, keepdims=True) + EPS)

    chunks = lambda x: x.reshape(nc, CHUNK, HEAD_DIM)
    q = chunks(l2n(conv_silu(q_ref, cq_ref, 0)) * (HEAD_DIM ** -0.5))
    k = chunks(l2n(conv_silu(k_ref, ck_ref, 1)))
    v = chunks(conv_silu(v_ref, cv_ref, 2))
    beta = chunks(_pick_lane(beta_ref[...], h))
    g = chunks(_pick_lane(g_ref[...], h))

    eg = jnp.exp(g)
    kb = k * beta
    vb = v * beta
    kg = kb * eg
    qg = q * eg
    g_last = g[:, CHUNK - 1, :]
    kt = k * jnp.exp(g_last[:, None, :] - g)
    chunk_decay = jnp.exp(g_last)

    pick = (lax.broadcasted_iota(jnp.int32, (nc, CHUNK, LANES), 1)
            == lax.broadcasted_iota(jnp.int32, (nc, CHUNK, LANES), 2)).astype(BF16)
    g_hi = g.astype(BF16)
    g_rem = g - g_hi.astype(F32)
    g_mid = g_rem.astype(BF16)
    g_lo = (g_rem - g_mid.astype(F32)).astype(BF16)
    g_row = _bdot_nt(pick, g_hi) + _bdot_nt(pick, g_mid) + _bdot_nt(pick, g_lo)
    ii = lax.broadcasted_iota(jnp.int32, (nc, CHUNK, CHUNK), 1)
    jj = lax.broadcasted_iota(jnp.int32, (nc, CHUNK, CHUNK), 2)
    incl = ii >= jj
    strict = ii > jj
    decay = jnp.where(incl, jnp.exp(jnp.minimum(g[:, :, :CHUNK] - g_row, 0.0)), 0.0)

    kbf = k.astype(BF16)
    m_mat = jnp.where(strict, _bdot_nt(kb.astype(BF16), kbf) * decay, 0.0)
    attn = _bdot_nt(q.astype(BF16), kbf) * decay

    t_mat = (ii == jj).astype(F32)
    s = 1
    while s < CHUNK:
        pair = ((ii & -(2 * s)) == (jj & -(2 * s))) & ((ii & s) != 0) & ((jj & s) == 0)
        t_mat = t_mat - _bdot(t_mat, _bdot(jnp.where(pair, m_mat, 0.0), t_mat))
        s *= 2
    u = _bdot(t_mat, vb)
    w = _bdot(t_mat, kg)

    state = state_ref[...]
    outs = []
    for c in range(nc):
        sb = state.astype(BF16)
        v_new = u[c] - _dot(w[c].astype(BF16), sb)
        vnb = v_new.astype(BF16)
        outs.append(_dot(qg[c].astype(BF16), sb) + _dot(attn[c].astype(BF16), vnb))
        state = state * chunk_decay[c:c + 1, :] + _dot_tn(kt[c].astype(BF16), vnb)
    state_ref[...] = state

    o = jnp.concatenate(outs, axis=0)
    o_ref[...] = (_rms(o) * ng_ref[...] * _silu(z_ref[...])).astype(o_ref.dtype)


def _gdn_core(proj, beta, g, conv_w, norm_g, batch, seq, t_rows=512):
    m = proj.shape[0]
    heads = conv_w.shape[1] // (3 * HEAD_DIM)
    t_rows = min(t_rows, seq)
    nt = seq // t_rows
    row = lambda b, h, t: b * nt + t
    col = lambda part: pl.BlockSpec((t_rows, HEAD_DIM), lambda b, h, t: (row(b, h, t), part * heads + h))
    cw = lambda part: pl.BlockSpec((conv_w.shape[0], HEAD_DIM), lambda b, h, t: (0, part * heads + h))
    small = pl.BlockSpec((t_rows, LANES), lambda b, h, t: (row(b, h, t), 0))
    return pl.pallas_call(
        functools.partial(_gdn_kernel, t_rows=t_rows),
        grid=(batch, heads, nt),
        in_specs=[col(0), col(1), col(2), col(3), small, small, cw(0), cw(1), cw(2),
                  pl.BlockSpec((1, HEAD_DIM), lambda b, h, t: (0, 0))],
        out_specs=pl.BlockSpec((t_rows, HEAD_DIM), lambda b, h, t: (row(b, h, t), h)),
        out_shape=jax.ShapeDtypeStruct((m, heads * HEAD_DIM), BF16),
        scratch_shapes=[pltpu.VMEM((HEAD_DIM, HEAD_DIM), F32), pltpu.VMEM((3, 8, HEAD_DIM), F32)],
        compiler_params=_params("arbitrary", "arbitrary", "arbitrary"),
        name="gdn_core",
    )(proj, proj, proj, proj, beta, g, conv_w, conv_w, conv_w, norm_g.reshape(1, HEAD_DIM))


def _fox_gates_kernel(a_ref, w_ref, bf_ref, cum_ref, cum_t_ref, carry_ref, *, tm, seq):
    i = pl.program_id(0)

    @pl.when((i * tm) % seq == 0)
    def _():
        carry_ref[...] = jnp.zeros(carry_ref.shape, F32)

    logit = _dot(a_ref[...], w_ref[...]) + bf_ref[...]
    log_f = -_softplus(-logit)
    cum = _prefix_sum_rows(log_f, tm) + carry_ref[...]
    carry_ref[...] = cum[tm - 1:tm, :]
    cum_ref[...] = cum
    cum_t_ref[...] = cum.T


def _fox_gates(a, w_small, b_f, batch, seq, tm=512):
    m, k = a.shape
    tm = min(tm, seq)
    nt = seq // tm
    return pl.pallas_call(
        functools.partial(_fox_gates_kernel, tm=tm, seq=seq),
        grid=(m // tm,),
        in_specs=[pl.BlockSpec((tm, k), lambda i: (i, 0)),
                  pl.BlockSpec((k, LANES), lambda i: (0, 0)),
                  pl.BlockSpec((1, LANES), lambda i: (0, 0))],
        out_specs=[pl.BlockSpec((tm, LANES), lambda i: (i, 0)),
                   pl.BlockSpec((None, LANES, tm), lambda i: (i // nt, 0, i % nt))],
        out_shape=[jax.ShapeDtypeStruct((m, LANES), F32),
                   jax.ShapeDtypeStruct((batch, LANES, seq), F32)],
        scratch_shapes=[pltpu.VMEM((1, LANES), F32)],
        compiler_params=_params("arbitrary"),
        name="fox_gates",
    )(a, w_small, b_f)


def _fox_kernel(q_ref, k_ref, v_ref, og_ref, cum_ref, cum_t_ref, qg_ref, kg_ref, o_ref,
                kn_ref, vb_ref, *, tq):
    h = pl.program_id(1)
    i = pl.program_id(2)

    @pl.when(i == 0)
    def _():
        kn_ref[...] = (_rms(k_ref[...]) * kg_ref[...]).astype(BF16)
        vb_ref[...] = v_ref[...].astype(BF16)

    qn = (_rms(q_ref[...]) * qg_ref[...] * (HEAD_DIM ** -0.5)).astype(BF16)
    cq = _pick_lane(cum_ref[...], h)
    cq = jnp.concatenate([cq] * (tq // LANES), axis=1)

    def block(j, carry, diagonal):
        m_prev, l_prev, acc = carry
        off = pl.multiple_of(j * tq, tq)
        s = _dot_nt(qn, kn_ref[pl.ds(off, tq), :])
        s = s + (cq - cum_t_ref[pl.ds(h, 1), pl.ds(off, tq)])
        if diagonal:
            rows = lax.broadcasted_iota(jnp.int32, (tq, tq), 0)
            cols = lax.broadcasted_iota(jnp.int32, (tq, tq), 1)
            s = jnp.where(rows >= cols, s, -jnp.inf)
        m_new = jnp.maximum(m_prev, jnp.max(s, axis=1, keepdims=True))
        p = jnp.exp(s - m_new)
        alpha = jnp.exp(m_prev - m_new)
        l_new = alpha * l_prev + jnp.sum(p, axis=1, keepdims=True)
        acc = alpha * acc + _dot(p.astype(BF16), vb_ref[pl.ds(off, tq), :])
        return m_new, l_new, acc

    init = (jnp.full((tq, 1), -jnp.inf, F32), jnp.zeros((tq, 1), F32), jnp.zeros((tq, HEAD_DIM), F32))
    carry = lax.fori_loop(0, i, lambda j, c: block(j, c, False), init)
    _, l_fin, acc = block(i, carry, True)
    o_ref[...] = (acc / l_fin * _sigmoid(og_ref[...])).astype(o_ref.dtype)


def _fox_core(proj, cum, cum_t, q_gain, k_gain, batch, seq, tq=512):
    m = proj.shape[0]
    heads = (proj.shape[1] // HEAD_DIM) // 4
    tq = min(tq, seq)
    nq = seq // tq
    row = lambda b, h, i: b * nq + i
    tile = lambda part: pl.BlockSpec((tq, HEAD_DIM), lambda b, h, i: (row(b, h, i), part * heads + h))
    full = lambda part: pl.BlockSpec((seq, HEAD_DIM), lambda b, h, i: (b, part * heads + h))
    gain = pl.BlockSpec((1, HEAD_DIM), lambda b, h, i: (0, 0))
    return pl.pallas_call(
        functools.partial(_fox_kernel, tq=tq),
        grid=(batch, heads, nq),
        in_specs=[tile(0), full(1), full(2), tile(3),
                  pl.BlockSpec((tq, LANES), lambda b, h, i: (row(b, h, i), 0)),
                  pl.BlockSpec((None, LANES, seq), lambda b, h, i: (b, 0, 0)),
                  gain, gain],
        out_specs=pl.BlockSpec((tq, HEAD_DIM), lambda b, h, i: (row(b, h, i), h)),
        out_shape=jax.ShapeDtypeStruct((m, heads * HEAD_DIM), BF16),
        scratch_shapes=[pltpu.VMEM((seq, HEAD_DIM), BF16), pltpu.VMEM((seq, HEAD_DIM), BF16)],
        compiler_params=_params("arbitrary", "arbitrary", "arbitrary"),
        name="fox_core",
    )(proj, proj, proj, proj, cum, cum_t, q_gain.reshape(1, HEAD_DIM), k_gain.reshape(1, HEAD_DIM))


def _pad_cols(w, width):
    return jnp.zeros((w.shape[0], width), w.dtype).at[:, :w.shape[1]].set(w)


def _pad_lanes(v):
    return jnp.zeros((1, LANES), F32).at[0, :v.shape[0]].set(v.astype(F32))


def kernel(x, c, ada_w, ada_b, norm_mix_g, norm_ffn_g, gdn_w_in, gdn_conv_w, gdn_A_log, gdn_dt_bias, gdn_norm_g, gdn_w_out, fox_w_in, fox_b_f, fox_q_norm_g, fox_k_norm_g, fox_w_out, ffn_w_up, ffn_conv_w, ffn_w_down, final_norm_g):
    batch, seq, d = x.shape
    depth = ada_w.shape[0]
    heads = d // HEAD_DIM
    mix = heads * HEAD_DIM
    m = batch * seq

    mod = _ada_mod(c, ada_w, ada_b)
    xf = x.reshape(m, d)
    for layer in range(depth):
        sh1, sc1, g1, sh2, sc2, g2 = (mod[layer, :batch, p * d:(p + 1) * d].reshape(batch, 1, d)
                                      for p in range(6))
        j = layer // 2
        hx = _norm_mod(xf, norm_mix_g[layer], sc1, sh1, seq)
        if layer % 2 == 0:
            w_in = gdn_w_in[j].astype(BF16)
            proj = _matmul(hx, w_in, 4 * mix, tm=1024, tn=512)
            w_small = jnp.concatenate([_pad_cols(w_in[:, 4 * mix:4 * mix + heads], LANES),
                                       _pad_cols(w_in[:, 4 * mix + heads:], LANES)], axis=1)
            beta, g = _gdn_gates(hx, w_small, _pad_lanes(gdn_A_log[j]), _pad_lanes(gdn_dt_bias[j]))
            mixed = _gdn_core(proj, beta, g, gdn_conv_w[j], gdn_norm_g[j], batch, seq)
            w_out = gdn_w_out[j]
        else:
            w_in = fox_w_in[j].astype(BF16)
            proj = _matmul(hx, w_in, 4 * mix, tm=1024, tn=512)
            cum, cum_t = _fox_gates(hx, _pad_cols(w_in[:, 4 * mix:], LANES), _pad_lanes(fox_b_f[j]),
                                    batch, seq)
            mixed = _fox_core(proj, cum, cum_t, fox_q_norm_g[j], fox_k_norm_g[j], batch, seq)
            w_out = fox_w_out[j]
        xf = _matmul_resid(mixed, w_out.astype(BF16), xf, g1, seq, tm=1024, tn=512)
        hx = _norm_mod(xf, norm_ffn_g[layer], sc2, sh2, seq)
        act = _ffn_up(hx, ffn_w_up[layer].astype(BF16), ffn_conv_w[layer], seq)
        xf = _matmul_resid(act, ffn_w_down[layer].astype(BF16), xf, g2, seq, tm=512, tn=256)
    return _final_norm(xf, final_norm_g).reshape(batch, seq, d)
```

```python
import functools

import jax
import jax.numpy as jnp
from jax import lax
from jax.experimental import pallas as pl
from jax.experimental.pallas import tpu as pltpu

EPS = 1e-6
CHUNK = 64
HEAD_DIM = 128
LANES = 128
CARRY = 8
VMEM_LIMIT_BYTES = 56 * 1024 * 1024

F32 = jnp.float32
BF16 = jnp.bfloat16
HIGHEST = lax.Precision.HIGHEST


def _params(*sem):
    return pltpu.CompilerParams(dimension_semantics=sem, vmem_limit_bytes=VMEM_LIMIT_BYTES)


def _sigmoid(x):
    return 1.0 / (1.0 + jnp.exp(-x))


def _silu(x):
    return x * _sigmoid(x)


def _softplus(x):
    return jnp.maximum(x, 0.0) + jnp.log(1.0 + jnp.exp(-jnp.abs(x)))


def _rms(x):
    return x * lax.rsqrt(jnp.mean(x * x, axis=-1, keepdims=True) + EPS)


def _dot(a, b):
    return jnp.dot(a, b, preferred_element_type=F32)


def _dot_nt(a, b):
    return lax.dot_general(a, b, (((1,), (1,)), ((), ())), preferred_element_type=F32)


def _dot_tn(a, b):
    return lax.dot_general(a, b, (((0,), (0,)), ((), ())), preferred_element_type=F32)


def _bdot(a, b):
    return lax.dot_general(a.astype(BF16), b.astype(BF16), (((2,), (1,)), ((0,), (0,))),
                           preferred_element_type=F32)


def _bdot_nt(a, b):
    return lax.dot_general(a, b, (((2,), (2,)), ((0,), (0,))), preferred_element_type=F32)


def _pick_lane(tile, lane):
    only = jnp.where(lax.broadcasted_iota(jnp.int32, tile.shape, 1) == lane, tile, 0.0)
    return jnp.broadcast_to(jnp.sum(only, axis=1, keepdims=True), tile.shape)


def _prefix_sum_rows(x, span):
    row = lax.broadcasted_iota(jnp.int32, x.shape, 0) & (span - 1)
    s = 1
    while s < span:
        x = x + jnp.where(row >= s, pltpu.roll(x, s, 0), 0.0)
        s *= 2
    return x


def _ada_kernel(c_ref, w_ref, b_ref, o_ref):
    c = c_ref[...]
    o_ref[...] = _dot(_silu(c).astype(BF16), w_ref[...].astype(BF16)) + b_ref[...]


def _ada_mod(c, ada_w, ada_b, tn=1024):
    depth, d, n = ada_w.shape
    rows = 8
    c_pad = jnp.zeros((rows, d), F32).at[:c.shape[0]].set(c)
    return pl.pallas_call(
        _ada_kernel,
        grid=(depth, n // tn),
        in_specs=[pl.BlockSpec((rows, d), lambda l, j: (0, 0)),
                  pl.BlockSpec((None, d, tn), lambda l, j: (l, 0, j)),
                  pl.BlockSpec((None, 1, tn), lambda l, j: (l, 0, j))],
        out_specs=pl.BlockSpec((None, rows, tn), lambda l, j: (l, 0, j)),
        out_shape=jax.ShapeDtypeStruct((depth, rows, n), F32),
        compiler_params=_params("arbitrary", "arbitrary"),
        name="ada_mod",
    )(c_pad, ada_w, ada_b.reshape(depth, 1, n))


def _norm_mod_kernel(x_ref, g_ref, sc_ref, sh_ref, o_ref):
    y = _rms(x_ref[...]) * g_ref[...]
    o_ref[...] = (y * (1.0 + sc_ref[...]) + sh_ref[...]).astype(o_ref.dtype)


def _norm_mod(x, g, sc, sh, seq, tm=256):
    m, d = x.shape
    row_vec = pl.BlockSpec((None, 1, d), lambda i: ((i * tm) // seq, 0, 0))
    return pl.pallas_call(
        _norm_mod_kernel,
        grid=(m // tm,),
        in_specs=[pl.BlockSpec((tm, d), lambda i: (i, 0)),
                  pl.BlockSpec((1, d), lambda i: (0, 0)), row_vec, row_vec],
        out_specs=pl.BlockSpec((tm, d), lambda i: (i, 0)),
        out_shape=jax.ShapeDtypeStruct((m, d), BF16),
        compiler_params=_params("arbitrary"),
        name="norm_mod",
    )(x, g.reshape(1, d), sc, sh)


def _final_norm_kernel(x_ref, g_ref, o_ref):
    o_ref[...] = _rms(x_ref[...]) * g_ref[...]


def _final_norm(x, g, tm=256):
    m, d = x.shape
    return pl.pallas_call(
        _final_norm_kernel,
        grid=(m // tm,),
        in_specs=[pl.BlockSpec((tm, d), lambda i: (i, 0)),
                  pl.BlockSpec((1, d), lambda i: (0, 0))],
        out_specs=pl.BlockSpec((tm, d), lambda i: (i, 0)),
        out_shape=jax.ShapeDtypeStruct((m, d), F32),
        compiler_params=_params("arbitrary"),
        name="final_norm",
    )(x, g.reshape(1, d))


def _mm_kernel(a_ref, w_ref, o_ref):
    o_ref[...] = _dot(a_ref[...], w_ref[...].astype(BF16)).astype(o_ref.dtype)


def _matmul(a, w_stack, layer, n_out, tm, tn, out_dtype=F32):
    m, k = a.shape
    tm, tn = min(tm, m), min(tn, n_out)
    return pl.pallas_call(
        _mm_kernel,
        grid=(n_out // tn, m // tm),
        in_specs=[pl.BlockSpec((tm, k), lambda j, i: (i, 0)),
                  pl.BlockSpec((None, k, tn), lambda j, i: (layer, 0, j))],
        out_specs=pl.BlockSpec((tm, tn), lambda j, i: (i, j)),
        out_shape=jax.ShapeDtypeStruct((m, n_out), out_dtype),
        compiler_params=_params("arbitrary", "arbitrary"),
        name="matmul",
    )(a, w_stack)


def _mm_resid_kernel(a_ref, w_ref, x_ref, g_ref, o_ref):
    o_ref[...] = x_ref[...] + g_ref[...] * _dot(a_ref[...], w_ref[...].astype(BF16))


def _matmul_resid(a, w_stack, layer, x, gate, seq, tm, tn, weight_resident):
    m, k = a.shape
    n = w_stack.shape[2]
    tm, tn = min(tm, m), min(tn, n)
    if weight_resident:
        grid = (n // tn, m // tm)
        ij = lambda f: (lambda j, i: f(i, j))
        a_spec = pl.BlockSpec((tm, k), ij(lambda i, j: (i, 0)))
    else:
        grid = (m // tm, n // tn)
        ij = lambda f: f
        a_spec = pl.BlockSpec((tm, k), lambda i, j: (i, 0), pipeline_mode=pl.Buffered(1))
    return pl.pallas_call(
        _mm_resid_kernel,
        grid=grid,
        in_specs=[a_spec,
                  pl.BlockSpec((None, k, tn), ij(lambda i, j: (layer, 0, j))),
                  pl.BlockSpec((tm, tn), ij(lambda i, j: (i, j))),
                  pl.BlockSpec((None, 1, tn), ij(lambda i, j: ((i * tm) // seq, 0, j)))],
        out_specs=pl.BlockSpec((tm, tn), ij(lambda i, j: (i, j))),
        out_shape=jax.ShapeDtypeStruct((m, n), F32),
        compiler_params=_params("arbitrary", "arbitrary"),
        name="matmul_resid",
    )(a, w_stack, x, gate)


def _ffn_up_kernel(a_ref, wg_ref, wv_ref, cg_ref, cv_ref, o_ref, carry_ref, *, tm, seq):
    i = pl.program_id(1)

    @pl.when((i * tm) % seq == 0)
    def _():
        carry_ref[...] = jnp.zeros(carry_ref.shape, F32)

    a = a_ref[...]

    def conv_branch(w_ref, c_ref, idx):
        p = _dot(a, w_ref[...].astype(BF16))
        pe = jnp.concatenate([carry_ref[idx], p], axis=0)
        carry_ref[idx] = p[tm - CARRY:]
        c = c_ref[...]
        y = c[0:1] * pltpu.roll(pe, 2, 0) + c[1:2] * pltpu.roll(pe, 1, 0) + c[2:3] * pe
        return y[CARRY:]

    gate = conv_branch(wg_ref, cg_ref, 0)
    val = conv_branch(wv_ref, cv_ref, 1)
    o_ref[...] = (_silu(gate) * val).astype(o_ref.dtype)


def _ffn_up(a, w_stack, conv_stack, layer, seq, tm=1024, tn=256):
    m, k = a.shape
    f = w_stack.shape[2] // 2
    taps = conv_stack.shape[1]
    tm = min(tm, seq)
    nj = f // tn
    return pl.pallas_call(
        functools.partial(_ffn_up_kernel, tm=tm, seq=seq),
        grid=(nj, m // tm),
        in_specs=[pl.BlockSpec((tm, k), lambda j, i: (i, 0)),
                  pl.BlockSpec((None, k, tn), lambda j, i: (layer, 0, j)),
                  pl.BlockSpec((None, k, tn), lambda j, i: (layer, 0, j + nj)),
                  pl.BlockSpec((None, taps, tn), lambda j, i: (layer, 0, j)),
                  pl.BlockSpec((None, taps, tn), lambda j, i: (layer, 0, j + nj))],
        out_specs=pl.BlockSpec((tm, tn), lambda j, i: (i, j)),
        out_shape=jax.ShapeDtypeStruct((m, f), BF16),
        scratch_shapes=[pltpu.VMEM((2, CARRY, tn), F32)],
        compiler_params=_params("arbitrary", "arbitrary"),
        name="ffn_up",
    )(a, w_stack, w_stack, conv_stack, conv_stack)


def _gdn_gates_kernel(a_ref, w_ref, alog_ref, dt_ref, beta_ref, g_ref):
    p = _dot(a_ref[...], w_ref[...])
    beta_ref[...] = _sigmoid(p[:, :LANES])
    log_a = -jnp.exp(alog_ref[...]) * _softplus(p[:, LANES:] + dt_ref[...])
    g_ref[...] = _prefix_sum_rows(log_a, CHUNK)


def _gdn_gates(a, w_small, a_log, dt_bias, tm=512):
    m, k = a.shape
    tm = min(tm, m)
    out = jax.ShapeDtypeStruct((m, LANES), F32)
    return pl.pallas_call(
        _gdn_gates_kernel,
        grid=(m // tm,),
        in_specs=[pl.BlockSpec((tm, k), lambda i: (i, 0)),
                  pl.BlockSpec((k, 2 * LANES), lambda i: (0, 0)),
                  pl.BlockSpec((1, LANES), lambda i: (0, 0)),
                  pl.BlockSpec((1, LANES), lambda i: (0, 0))],
        out_specs=[pl.BlockSpec((tm, LANES), lambda i: (i, 0))] * 2,
        out_shape=[out, out],
        compiler_params=_params("arbitrary"),
        name="gdn_gates",
    )(a, w_small, a_log, dt_bias)


def _gdn_kernel(q_ref, k_ref, v_ref, z_ref, beta_ref, g_ref, cq_ref, ck_ref, cv_ref, ng_ref,
                o_ref, state_ref, carry_ref, *, t_rows, heads_per_step):
    @pl.when(pl.program_id(2) == 0)
    def _():
        state_ref[...] = jnp.zeros(state_ref.shape, F32)
        carry_ref[...] = jnp.zeros(carry_ref.shape, F32)

    hb = heads_per_step
    ncl = t_rows // CHUNK
    nc = hb * ncl
    head_lanes = [slice(hh * HEAD_DIM, (hh + 1) * HEAD_DIM) for hh in range(hb)]

    def conv_silu(x_ref, c_ref, idx, hh):
        x = x_ref[:, head_lanes[hh]]
        xe = jnp.concatenate([carry_ref[idx, hh], x], axis=0)
        carry_ref[idx, hh] = x[t_rows - CARRY:]
        c = c_ref[:, head_lanes[hh]]
        y = (c[0:1] * pltpu.roll(xe, 3, 0) + c[1:2] * pltpu.roll(xe, 2, 0)
             + c[2:3] * pltpu.roll(xe, 1, 0) + c[3:4] * xe)
        return _silu(y[CARRY:])

    def l2n(x):
        return x * lax.rsqrt(jnp.sum(x * x, axis=-1, keepdims=True) + EPS)

    def chunks(per_head):
        return jnp.concatenate([per_head(hh) for hh in range(hb)], axis=0).reshape(nc, CHUNK, HEAD_DIM)

    head0 = pl.program_id(1) * hb
    q = chunks(lambda hh: l2n(conv_silu(q_ref, cq_ref, 0, hh)) * (HEAD_DIM ** -0.5))
    k = chunks(lambda hh: l2n(conv_silu(k_ref, ck_ref, 1, hh)))
    v = chunks(lambda hh: conv_silu(v_ref, cv_ref, 2, hh))
    beta = chunks(lambda hh: _pick_lane(beta_ref[...], head0 + hh))
    g = chunks(lambda hh: _pick_lane(g_ref[...], head0 + hh))

    eg = jnp.exp(g)
    kb = k * beta
    vb = v * beta
    kg = kb * eg
    qg = q * eg
    g_last = g[:, CHUNK - 1, :]
    kt = k * jnp.exp(g_last[:, None, :] - g)
    chunk_decay = jnp.exp(g_last)

    pick = (lax.broadcasted_iota(jnp.int32, (nc, CHUNK, LANES), 1)
            == lax.broadcasted_iota(jnp.int32, (nc, CHUNK, LANES), 2)).astype(BF16)
    g_hi = g.astype(BF16)
    g_rem = g - g_hi.astype(F32)
    g_mid = g_rem.astype(BF16)
    g_lo = (g_rem - g_mid.astype(F32)).astype(BF16)
    g_row = _bdot_nt(pick, g_hi) + _bdot_nt(pick, g_mid) + _bdot_nt(pick, g_lo)
    ii = lax.broadcasted_iota(jnp.int32, (CHUNK, CHUNK), 0)
    jj = lax.broadcasted_iota(jnp.int32, (CHUNK, CHUNK), 1)
    incl = (ii >= jj)[None]
    strict = (ii > jj)[None]
    decay = jnp.where(incl, jnp.exp(jnp.minimum(g[:, :, :CHUNK] - g_row, 0.0)), 0.0)

    kbf = k.astype(BF16)
    m_mat = jnp.where(strict, _bdot_nt(kb.astype(BF16), kbf) * decay, 0.0)
    attn = _bdot_nt(q.astype(BF16), kbf) * decay

    t_mat = jnp.broadcast_to((ii == jj).astype(F32)[None], m_mat.shape)
    s = 1
    while s < CHUNK:
        pair = ((ii & -(2 * s)) == (jj & -(2 * s))) & ((ii & s) != 0) & ((jj & s) == 0)
        t_mat = t_mat - _bdot(t_mat, _bdot(jnp.where(pair[None], m_mat, 0.0), t_mat))
        s *= 2
    u = _bdot(t_mat, vb)
    w = _bdot(t_mat, kg)

    ub, wb, ktb, attn_b = u.astype(BF16), w.astype(BF16), kt.astype(BF16), attn.astype(BF16)
    out_lhs = (qg - _bdot(attn_b, wb)).astype(BF16)
    out_add = _bdot(attn_b, ub)
    states = [state_ref[hh] for hh in range(hb)]
    outs = [[] for _ in range(hb)]
    for cl in range(ncl):
        for hh in range(hb):
            c = hh * ncl + cl
            sb = states[hh].astype(BF16)
            outs[hh].append(_dot(out_lhs[c], sb) + out_add[c])
            feedback = _dot_tn(ktb[c], wb[c]).astype(BF16)
            states[hh] = (states[hh] * chunk_decay[c:c + 1, :]
                          + (_dot_tn(ktb[c], ub[c]) - _dot(feedback, sb)))
    for hh in range(hb):
        state_ref[hh] = states[hh]
        o = jnp.concatenate(outs[hh], axis=0)
        o_ref[:, head_lanes[hh]] = (_rms(o) * ng_ref[...] * _silu(z_ref[:, head_lanes[hh]])).astype(o_ref.dtype)


def _gdn_core(proj, beta, g, conv_w, norm_g, batch, seq, t_rows=512, heads_per_step=4):
    m = proj.shape[0]
    heads = conv_w.shape[1] // (3 * HEAD_DIM)
    t_rows = min(t_rows, seq)
    nt = seq // t_rows
    hb = heads_per_step
    width = hb * HEAD_DIM
    groups = heads // hb
    row = lambda b, h, t: b * nt + t
    col = lambda part: pl.BlockSpec((t_rows, width), lambda b, h, t: (row(b, h, t), part * groups + h))
    cw = lambda part: pl.BlockSpec((conv_w.shape[0], width), lambda b, h, t: (0, part * groups + h))
    small = pl.BlockSpec((t_rows, LANES), lambda b, h, t: (row(b, h, t), 0))
    return pl.pallas_call(
        functools.partial(_gdn_kernel, t_rows=t_rows, heads_per_step=hb),
        grid=(batch, groups, nt),
        in_specs=[col(0), col(1), col(2), col(3), small, small, cw(0), cw(1), cw(2),
                  pl.BlockSpec((1, HEAD_DIM), lambda b, h, t: (0, 0))],
        out_specs=pl.BlockSpec((t_rows, width), lambda b, h, t: (row(b, h, t), h)),
        out_shape=jax.ShapeDtypeStruct((m, heads * HEAD_DIM), BF16),
        scratch_shapes=[pltpu.VMEM((hb, HEAD_DIM, HEAD_DIM), F32),
                        pltpu.VMEM((3, hb, CARRY, HEAD_DIM), F32)],
        compiler_params=_params("arbitrary", "arbitrary", "arbitrary"),
        name="gdn_core",
    )(proj, proj, proj, proj, beta, g, conv_w, conv_w, conv_w, norm_g.reshape(1, HEAD_DIM))


def _fox_gates_kernel(a_ref, w_ref, bf_ref, cum_ref, cum_t_ref, carry_ref, *, tm, seq):
    i = pl.program_id(0)

    @pl.when((i * tm) % seq == 0)
    def _():
        carry_ref[...] = jnp.zeros(carry_ref.shape, F32)

    logit = _dot(a_ref[...], w_ref[...]) + bf_ref[...]
    log_f = -_softplus(-logit)
    cum = _prefix_sum_rows(log_f, tm) + carry_ref[...]
    carry_ref[...] = cum[tm - 1:tm, :]
    cum_ref[...] = cum
    cum_t_ref[...] = cum.T


def _fox_gates(a, w_small, b_f, batch, seq, tm=512):
    m, k = a.shape
    tm = min(tm, seq)
    nt = seq // tm
    return pl.pallas_call(
        functools.partial(_fox_gates_kernel, tm=tm, seq=seq),
        grid=(m // tm,),
        in_specs=[pl.BlockSpec((tm, k), lambda i: (i, 0)),
                  pl.BlockSpec((k, LANES), lambda i: (0, 0)),
                  pl.BlockSpec((1, LANES), lambda i: (0, 0))],
        out_specs=[pl.BlockSpec((tm, LANES), lambda i: (i, 0)),
                   pl.BlockSpec((None, LANES, tm), lambda i: (i // nt, 0, i % nt))],
        out_shape=[jax.ShapeDtypeStruct((m, LANES), F32),
                   jax.ShapeDtypeStruct((batch, LANES, seq), F32)],
        scratch_shapes=[pltpu.VMEM((1, LANES), F32)],
        compiler_params=_params("arbitrary"),
        name="fox_gates",
    )(a, w_small, b_f)


def _fox_kernel(q_ref, k_ref, v_ref, og_ref, cum_ref, cum_t_ref, qg_ref, kg_ref, o_ref,
                kn_ref, vb_ref, *, tq):
    h = pl.program_id(1)
    i = pl.program_id(2)

    @pl.when(i == 0)
    def _():
        kn_ref[...] = (_rms(k_ref[...]) * kg_ref[...]).astype(BF16)
        vb_ref[...] = v_ref[...].astype(BF16)

    qn = (_rms(q_ref[...]) * qg_ref[...] * (HEAD_DIM ** -0.5)).astype(BF16)
    cq = _pick_lane(cum_ref[...], h)
    cq = jnp.concatenate([cq] * (tq // LANES), axis=1)

    def block(j, carry, diagonal):
        m_prev, l_prev, acc = carry
        off = pl.multiple_of(j * tq, tq)
        s = _dot_nt(qn, kn_ref[pl.ds(off, tq), :])
        s = s + (cq - cum_t_ref[pl.ds(h, 1), pl.ds(off, tq)])
        if diagonal:
            rows = lax.broadcasted_iota(jnp.int32, (tq, tq), 0)
            cols = lax.broadcasted_iota(jnp.int32, (tq, tq), 1)
            s = jnp.where(rows >= cols, s, -jnp.inf)
        m_new = jnp.maximum(m_prev, jnp.max(s, axis=1, keepdims=True))
        p = jnp.exp(s - m_new)
        alpha = jnp.exp(m_prev - m_new)
        l_new = alpha * l_prev + jnp.sum(p, axis=1, keepdims=True)
        acc = alpha * acc + _dot(p.astype(BF16), vb_ref[pl.ds(off, tq), :])
        return m_new, l_new, acc

    init = (jnp.full((tq, 1), -jnp.inf, F32), jnp.zeros((tq, 1), F32), jnp.zeros((tq, HEAD_DIM), F32))
    carry = lax.fori_loop(0, i, lambda j, c: block(j, c, False), init)
    _, l_fin, acc = block(i, carry, True)
    o_ref[...] = (acc / l_fin * _sigmoid(og_ref[...])).astype(o_ref.dtype)


def _fox_core(proj, cum, cum_t, q_gain, k_gain, batch, seq, tq=512):
    m = proj.shape[0]
    heads = (proj.shape[1] // HEAD_DIM) // 4
    tq = min(tq, seq)
    nq = seq // tq
    row = lambda b, h, i: b * nq + i
    tile = lambda part: pl.BlockSpec((tq, HEAD_DIM), lambda b, h, i: (row(b, h, i), part * heads + h))
    full = lambda part: pl.BlockSpec((seq, HEAD_DIM), lambda b, h, i: (b, part * heads + h))
    gain = pl.BlockSpec((1, HEAD_DIM), lambda b, h, i: (0, 0))
    return pl.pallas_call(
        functools.partial(_fox_kernel, tq=tq),
        grid=(batch, heads, nq),
        in_specs=[tile(0), full(1), full(2), tile(3),
                  pl.BlockSpec((tq, LANES), lambda b, h, i: (row(b, h, i), 0)),
                  pl.BlockSpec((None, LANES, seq), lambda b, h, i: (b, 0, 0)),
                  gain, gain],
        out_specs=pl.BlockSpec((tq, HEAD_DIM), lambda b, h, i: (row(b, h, i), h)),
        out_shape=jax.ShapeDtypeStruct((m, heads * HEAD_DIM), BF16),
        scratch_shapes=[pltpu.VMEM((seq, HEAD_DIM), BF16), pltpu.VMEM((seq, HEAD_DIM), BF16)],
        compiler_params=_params("arbitrary", "arbitrary", "arbitrary"),
        name="fox_core",
    )(proj, proj, proj, proj, cum, cum_t, q_gain.reshape(1, HEAD_DIM), k_gain.reshape(1, HEAD_DIM))


def _pad_cols(w, width):
    return jnp.zeros((w.shape[0], width), w.dtype).at[:, :w.shape[1]].set(w)


def _pad_lanes(v):
    return jnp.zeros((1, LANES), F32).at[0, :v.shape[0]].set(v.astype(F32))


def kernel(x, c, ada_w, ada_b, norm_mix_g, norm_ffn_g, gdn_w_in, gdn_conv_w, gdn_A_log, gdn_dt_bias, gdn_norm_g, gdn_w_out, fox_w_in, fox_b_f, fox_q_norm_g, fox_k_norm_g, fox_w_out, ffn_w_up, ffn_conv_w, ffn_w_down, final_norm_g):
    batch, seq, d = x.shape
    depth = ada_w.shape[0]
    heads = d // HEAD_DIM
    mix = heads * HEAD_DIM
    m = batch * seq

    mod = _ada_mod(c, ada_w, ada_b)
    xf = x.reshape(m, d)
    for layer in range(depth):
        sh1, sc1, g1, sh2, sc2, g2 = (mod[layer, :batch, p * d:(p + 1) * d].reshape(batch, 1, d)
                                      for p in range(6))
        j = layer // 2
        hx = _norm_mod(xf, norm_mix_g[layer], sc1, sh1, seq)
        if layer % 2 == 0:
            proj = _matmul(hx, gdn_w_in, j, 4 * mix, tm=1024, tn=512)
            gate_w = gdn_w_in[j, :, 4 * mix:].astype(BF16)
            w_small = jnp.concatenate([_pad_cols(gate_w[:, :heads], LANES),
                                       _pad_cols(gate_w[:, heads:], LANES)], axis=1)
            beta, g = _gdn_gates(hx, w_small, _pad_lanes(gdn_A_log[j]), _pad_lanes(gdn_dt_bias[j]))
            mixed = _gdn_core(proj, beta, g, gdn_conv_w[j], gdn_norm_g[j], batch, seq)
            w_out = gdn_w_out
        else:
            proj = _matmul(hx, fox_w_in, j, 4 * mix, tm=1024, tn=512)
            gate_w = fox_w_in[j, :, 4 * mix:].astype(BF16)
            cum, cum_t = _fox_gates(hx, _pad_cols(gate_w, LANES), _pad_lanes(fox_b_f[j]), batch, seq)
            mixed = _fox_core(proj, cum, cum_t, fox_q_norm_g[j], fox_k_norm_g[j], batch, seq)
            w_out = fox_w_out
        xf = _matmul_resid(mixed, w_out, j, xf, g1, seq, tm=1024, tn=512, weight_resident=True)
        hx = _norm_mod(xf, norm_ffn_g[layer], sc2, sh2, seq)
        act = _ffn_up(hx, ffn_w_up, ffn_conv_w, layer, seq)
        xf = _matmul_resid(act, ffn_w_down, layer, xf, g2, seq, tm=1024, tn=256, weight_resident=False)
    return _final_norm(xf, final_norm_g).reshape(batch, seq, d)
```

```python
import functools

import jax
import jax.numpy as jnp
from jax import lax
from jax.experimental import pallas as pl
from jax.experimental.pallas import tpu as pltpu

EPS = 1e-6
CHUNK = 64
HEAD_DIM = 128
LANES = 128
CARRY = 8
VMEM_LIMIT_BYTES = 56 * 1024 * 1024

F32 = jnp.float32
BF16 = jnp.bfloat16
HIGHEST = lax.Precision.HIGHEST


def _params(*sem):
    return pltpu.CompilerParams(dimension_semantics=sem, vmem_limit_bytes=VMEM_LIMIT_BYTES)


def _sigmoid(x):
    return 1.0 / (1.0 + jnp.exp(-x))


def _silu(x):
    return x * _sigmoid(x)


def _softplus(x):
    return jnp.maximum(x, 0.0) + jnp.log(1.0 + jnp.exp(-jnp.abs(x)))


def _rms(x):
    return x * lax.rsqrt(jnp.mean(x * x, axis=-1, keepdims=True) + EPS)


def _dot(a, b):
    return jnp.dot(a, b, preferred_element_type=F32)


def _dot_nt(a, b):
    return lax.dot_general(a, b, (((1,), (1,)), ((), ())), preferred_element_type=F32)


def _dot_tn(a, b):
    return lax.dot_general(a, b, (((0,), (0,)), ((), ())), preferred_element_type=F32)


def _bdot(a, b):
    return lax.dot_general(a.astype(BF16), b.astype(BF16), (((2,), (1,)), ((0,), (0,))),
                           preferred_element_type=F32)


def _bdot_nt(a, b):
    return lax.dot_general(a, b, (((2,), (2,)), ((0,), (0,))), preferred_element_type=F32)


def _pick_lane(tile, lane):
    only = jnp.where(lax.broadcasted_iota(jnp.int32, tile.shape, 1) == lane, tile, 0.0)
    return jnp.broadcast_to(jnp.sum(only, axis=1, keepdims=True), tile.shape)


def _prefix_sum_rows(x, span):
    row = lax.broadcasted_iota(jnp.int32, x.shape, 0) & (span - 1)
    s = 1
    while s < span:
        x = x + jnp.where(row >= s, pltpu.roll(x, s, 0), 0.0)
        s *= 2
    return x


def _ada_kernel(c_ref, w_ref, b_ref, o_ref):
    c = c_ref[...]
    o_ref[...] = _dot(_silu(c).astype(BF16), w_ref[...].astype(BF16)) + b_ref[...]


def _ada_mod(c, ada_w, ada_b, tn=1024):
    depth, d, n = ada_w.shape
    rows = 8
    c_pad = jnp.zeros((rows, d), F32).at[:c.shape[0]].set(c)
    return pl.pallas_call(
        _ada_kernel,
        grid=(depth, n // tn),
        in_specs=[pl.BlockSpec((rows, d), lambda l, j: (0, 0)),
                  pl.BlockSpec((None, d, tn), lambda l, j: (l, 0, j)),
                  pl.BlockSpec((None, 1, tn), lambda l, j: (l, 0, j))],
        out_specs=pl.BlockSpec((None, rows, tn), lambda l, j: (l, 0, j)),
        out_shape=jax.ShapeDtypeStruct((depth, rows, n), F32),
        compiler_params=_params("arbitrary", "arbitrary"),
        name="ada_mod",
    )(c_pad, ada_w, ada_b.reshape(depth, 1, n))


def _norm_mod_kernel(x_ref, g_ref, sc_ref, sh_ref, o_ref):
    y = _rms(x_ref[...]) * g_ref[...]
    o_ref[...] = (y * (1.0 + sc_ref[...]) + sh_ref[...]).astype(o_ref.dtype)


def _norm_mod(x, g, sc, sh, seq, tm=256):
    m, d = x.shape
    row_vec = pl.BlockSpec((None, 1, d), lambda i: ((i * tm) // seq, 0, 0))
    return pl.pallas_call(
        _norm_mod_kernel,
        grid=(m // tm,),
        in_specs=[pl.BlockSpec((tm, d), lambda i: (i, 0)),
                  pl.BlockSpec((1, d), lambda i: (0, 0)), row_vec, row_vec],
        out_specs=pl.BlockSpec((tm, d), lambda i: (i, 0)),
        out_shape=jax.ShapeDtypeStruct((m, d), BF16),
        compiler_params=_params("arbitrary"),
        name="norm_mod",
    )(x, g.reshape(1, d), sc, sh)


def _final_norm_kernel(x_ref, g_ref, o_ref):
    o_ref[...] = _rms(x_ref[...]) * g_ref[...]


def _final_norm(x, g, tm=256):
    m, d = x.shape
    return pl.pallas_call(
        _final_norm_kernel,
        grid=(m // tm,),
        in_specs=[pl.BlockSpec((tm, d), lambda i: (i, 0)),
                  pl.BlockSpec((1, d), lambda i: (0, 0))],
        out_specs=pl.BlockSpec((tm, d), lambda i: (i, 0)),
        out_shape=jax.ShapeDtypeStruct((m, d), F32),
        compiler_params=_params("arbitrary"),
        name="final_norm",
    )(x, g.reshape(1, d))


def _mm_kernel(a_ref, w_ref, o_ref):
    o_ref[...] = _dot(a_ref[...], w_ref[...].astype(BF16)).astype(o_ref.dtype)


def _matmul(a, w_stack, layer, n_out, tm, tn, out_dtype=F32):
    m, k = a.shape
    tm, tn = min(tm, m), min(tn, n_out)
    return pl.pallas_call(
        _mm_kernel,
        grid=(n_out // tn, m // tm),
        in_specs=[pl.BlockSpec((tm, k), lambda j, i: (i, 0)),
                  pl.BlockSpec((None, k, tn), lambda j, i: (layer, 0, j))],
        out_specs=pl.BlockSpec((tm, tn), lambda j, i: (i, j)),
        out_shape=jax.ShapeDtypeStruct((m, n_out), out_dtype),
        compiler_params=_params("arbitrary", "arbitrary"),
        name="matmul",
    )(a, w_stack)


def _mm_resid_kernel(a_ref, w_ref, x_ref, g_ref, o_ref):
    o_ref[...] = x_ref[...] + g_ref[...] * _dot(a_ref[...], w_ref[...].astype(BF16))


def _matmul_resid(a, w_stack, layer, x, gate, seq, tm, tn, weight_resident):
    m, k = a.shape
    n = w_stack.shape[2]
    tm, tn = min(tm, m), min(tn, n)
    if weight_resident:
        grid = (n // tn, m // tm)
        ij = lambda f: (lambda j, i: f(i, j))
        a_spec = pl.BlockSpec((tm, k), ij(lambda i, j: (i, 0)))
    else:
        grid = (m // tm, n // tn)
        ij = lambda f: f
        a_spec = pl.BlockSpec((tm, k), lambda i, j: (i, 0), pipeline_mode=pl.Buffered(1))
    return pl.pallas_call(
        _mm_resid_kernel,
        grid=grid,
        in_specs=[a_spec,
                  pl.BlockSpec((None, k, tn), ij(lambda i, j: (layer, 0, j))),
                  pl.BlockSpec((tm, tn), ij(lambda i, j: (i, j))),
                  pl.BlockSpec((None, 1, tn), ij(lambda i, j: ((i * tm) // seq, 0, j)))],
        out_specs=pl.BlockSpec((tm, tn), ij(lambda i, j: (i, j))),
        out_shape=jax.ShapeDtypeStruct((m, n), F32),
        compiler_params=_params("arbitrary", "arbitrary"),
        name="matmul_resid",
    )(a, w_stack, x, gate)


def _ffn_up_kernel(a_ref, wg_ref, wv_ref, cg_ref, cv_ref, o_ref, carry_ref, *, tm, seq):
    i = pl.program_id(1)

    @pl.when((i * tm) % seq == 0)
    def _():
        carry_ref[...] = jnp.zeros(carry_ref.shape, F32)

    a = a_ref[...]

    def conv_branch(w_ref, c_ref, idx):
        p = _dot(a, w_ref[...].astype(BF16))
        pe = jnp.concatenate([carry_ref[idx], p], axis=0)
        carry_ref[idx] = p[tm - CARRY:]
        c = c_ref[...]
        y = c[0:1] * pltpu.roll(pe, 2, 0) + c[1:2] * pltpu.roll(pe, 1, 0) + c[2:3] * pe
        return y[CARRY:]

    gate = conv_branch(wg_ref, cg_ref, 0)
    val = conv_branch(wv_ref, cv_ref, 1)
    o_ref[...] = (_silu(gate) * val).astype(o_ref.dtype)


def _ffn_up(a, w_stack, conv_stack, layer, seq, tm=1024, tn=256):
    m, k = a.shape
    f = w_stack.shape[2] // 2
    taps = conv_stack.shape[1]
    tm = min(tm, seq)
    nj = f // tn
    return pl.pallas_call(
        functools.partial(_ffn_up_kernel, tm=tm, seq=seq),
        grid=(nj, m // tm),
        in_specs=[pl.BlockSpec((tm, k), lambda j, i: (i, 0)),
                  pl.BlockSpec((None, k, tn), lambda j, i: (layer, 0, j)),
                  pl.BlockSpec((None, k, tn), lambda j, i: (layer, 0, j + nj)),
                  pl.BlockSpec((None, taps, tn), lambda j, i: (layer, 0, j)),
                  pl.BlockSpec((None, taps, tn), lambda j, i: (layer, 0, j + nj))],
        out_specs=pl.BlockSpec((tm, tn), lambda j, i: (i, j)),
        out_shape=jax.ShapeDtypeStruct((m, f), BF16),
        scratch_shapes=[pltpu.VMEM((2, CARRY, tn), F32)],
        compiler_params=_params("arbitrary", "arbitrary"),
        name="ffn_up",
    )(a, w_stack, w_stack, conv_stack, conv_stack)


def _gdn_gates_kernel(a_ref, w_ref, alog_ref, dt_ref, beta_ref, g_ref):
    p = _dot(a_ref[...], w_ref[...])
    beta_ref[...] = _sigmoid(p[:, :LANES])
    log_a = -jnp.exp(alog_ref[...]) * _softplus(p[:, LANES:] + dt_ref[...])
    g_ref[...] = _prefix_sum_rows(log_a, CHUNK)


def _gdn_gates(a, w_small, a_log, dt_bias, tm=512):
    m, k = a.shape
    tm = min(tm, m)
    out = jax.ShapeDtypeStruct((m, LANES), F32)
    return pl.pallas_call(
        _gdn_gates_kernel,
        grid=(m // tm,),
        in_specs=[pl.BlockSpec((tm, k), lambda i: (i, 0)),
                  pl.BlockSpec((k, 2 * LANES), lambda i: (0, 0)),
                  pl.BlockSpec((1, LANES), lambda i: (0, 0)),
                  pl.BlockSpec((1, LANES), lambda i: (0, 0))],
        out_specs=[pl.BlockSpec((tm, LANES), lambda i: (i, 0))] * 2,
        out_shape=[out, out],
        compiler_params=_params("arbitrary"),
        name="gdn_gates",
    )(a, w_small, a_log, dt_bias)


def _gdn_kernel(q_ref, k_ref, v_ref, z_ref, beta_ref, g_ref, cq_ref, ck_ref, cv_ref, ng_ref,
                o_ref, state_ref, carry_ref, *, t_rows, heads_per_step):
    @pl.when(pl.program_id(2) == 0)
    def _():
        state_ref[...] = jnp.zeros(state_ref.shape, F32)
        carry_ref[...] = jnp.zeros(carry_ref.shape, F32)

    hb = heads_per_step
    ncl = t_rows // CHUNK
    nc = hb * ncl
    head_lanes = [slice(hh * HEAD_DIM, (hh + 1) * HEAD_DIM) for hh in range(hb)]

    def conv_silu(x_ref, c_ref, idx, hh):
        x = x_ref[:, head_lanes[hh]]
        xe = jnp.concatenate([carry_ref[idx, hh], x], axis=0)
        carry_ref[idx, hh] = x[t_rows - CARRY:]
        c = c_ref[:, head_lanes[hh]]
        y = (c[0:1] * pltpu.roll(xe, 3, 0) + c[1:2] * pltpu.roll(xe, 2, 0)
             + c[2:3] * pltpu.roll(xe, 1, 0) + c[3:4] * xe)
        return _silu(y[CARRY:])

    def l2n(x):
        return x * lax.rsqrt(jnp.sum(x * x, axis=-1, keepdims=True) + EPS)

    def chunks(per_head):
        return jnp.concatenate([per_head(hh) for hh in range(hb)], axis=0).reshape(nc, CHUNK, HEAD_DIM)

    head0 = pl.program_id(1) * hb
    q = chunks(lambda hh: l2n(conv_silu(q_ref, cq_ref, 0, hh)) * (HEAD_DIM ** -0.5))
    k = chunks(lambda hh: l2n(conv_silu(k_ref, ck_ref, 1, hh)))
    v = chunks(lambda hh: conv_silu(v_ref, cv_ref, 2, hh))
    beta = chunks(lambda hh: _pick_lane(beta_ref[...], head0 + hh))
    g = chunks(lambda hh: _pick_lane(g_ref[...], head0 + hh))

    eg = jnp.exp(g)
    kb = k * beta
    vb = v * beta
    kg = kb * eg
    qg = q * eg
    g_last = g[:, CHUNK - 1, :]
    kt = k * jnp.exp(g_last[:, None, :] - g)
    chunk_decay = jnp.exp(g_last)

    pick = (lax.broadcasted_iota(jnp.int32, (nc, CHUNK, LANES), 1)
            == lax.broadcasted_iota(jnp.int32, (nc, CHUNK, LANES), 2)).astype(BF16)
    g_hi = g.astype(BF16)
    g_rem = g - g_hi.astype(F32)
    g_mid = g_rem.astype(BF16)
    g_lo = (g_rem - g_mid.astype(F32)).astype(BF16)
    g_row = _bdot_nt(pick, g_hi) + _bdot_nt(pick, g_mid) + _bdot_nt(pick, g_lo)
    ii = lax.broadcasted_iota(jnp.int32, (CHUNK, CHUNK), 0)
    jj = lax.broadcasted_iota(jnp.int32, (CHUNK, CHUNK), 1)
    incl = (ii >= jj)[None]
    strict = (ii > jj)[None]
    decay = jnp.where(incl, jnp.exp(jnp.minimum(g[:, :, :CHUNK] - g_row, 0.0)), 0.0)

    kbf = k.astype(BF16)
    m_mat = jnp.where(strict, _bdot_nt(kb.astype(BF16), kbf) * decay, 0.0)
    attn = _bdot_nt(q.astype(BF16), kbf) * decay

    t_mat = jnp.broadcast_to((ii == jj).astype(F32)[None], m_mat.shape)
    s = 1
    while s < CHUNK:
        pair = ((ii & -(2 * s)) == (jj & -(2 * s))) & ((ii & s) != 0) & ((jj & s) == 0)
        t_mat = t_mat - _bdot(t_mat, _bdot(jnp.where(pair[None], m_mat, 0.0), t_mat))
        s *= 2
    u = _bdot(t_mat, vb)
    w = _bdot(t_mat, kg)

    ub, wb, ktb, attn_b = u.astype(BF16), w.astype(BF16), kt.astype(BF16), attn.astype(BF16)
    out_lhs = (qg - _bdot(attn_b, wb)).astype(BF16)
    out_add = _bdot(attn_b, ub)
    states = [state_ref[hh] for hh in range(hb)]
    outs = [[] for _ in range(hb)]
    for cl in range(ncl):
        for hh in range(hb):
            c = hh * ncl + cl
            sb = states[hh].astype(BF16)
            outs[hh].append(_dot(out_lhs[c], sb) + out_add[c])
            feedback = _dot_tn(ktb[c], wb[c]).astype(BF16)
            states[hh] = (states[hh] * chunk_decay[c:c + 1, :]
                          + (_dot_tn(ktb[c], ub[c]) - _dot(feedback, sb)))
    for hh in range(hb):
        state_ref[hh] = states[hh]
        o = jnp.concatenate(outs[hh], axis=0)
        o_ref[:, head_lanes[hh]] = (_rms(o) * ng_ref[...] * _silu(z_ref[:, head_lanes[hh]])).astype(o_ref.dtype)


def _gdn_core(proj, beta, g, conv_w, norm_g, batch, seq, t_rows=512, heads_per_step=4):
    m = proj.shape[0]
    heads = conv_w.shape[1] // (3 * HEAD_DIM)
    t_rows = min(t_rows, seq)
    nt = seq // t_rows
    hb = heads_per_step
    width = hb * HEAD_DIM
    groups = heads // hb
    row = lambda b, h, t: b * nt + t
    col = lambda part: pl.BlockSpec((t_rows, width), lambda b, h, t: (row(b, h, t), part * groups + h))
    cw = lambda part: pl.BlockSpec((conv_w.shape[0], width), lambda b, h, t: (0, part * groups + h))
    small = pl.BlockSpec((t_rows, LANES), lambda b, h, t: (row(b, h, t), 0))
    return pl.pallas_call(
        functools.partial(_gdn_kernel, t_rows=t_rows, heads_per_step=hb),
        grid=(batch, groups, nt),
        in_specs=[col(0), col(1), col(2), col(3), small, small, cw(0), cw(1), cw(2),
                  pl.BlockSpec((1, HEAD_DIM), lambda b, h, t: (0, 0))],
        out_specs=pl.BlockSpec((t_rows, width), lambda b, h, t: (row(b, h, t), h)),
        out_shape=jax.ShapeDtypeStruct((m, heads * HEAD_DIM), BF16),
        scratch_shapes=[pltpu.VMEM((hb, HEAD_DIM, HEAD_DIM), F32),
                        pltpu.VMEM((3, hb, CARRY, HEAD_DIM), F32)],
        compiler_params=_params("arbitrary", "arbitrary", "arbitrary"),
        name="gdn_core",
    )(proj, proj, proj, proj, beta, g, conv_w, conv_w, conv_w, norm_g.reshape(1, HEAD_DIM))


def _fox_gates_kernel(a_ref, w_ref, bf_ref, cum_ref, carry_ref, *, tm, seq):
    i = pl.program_id(0)

    @pl.when((i * tm) % seq == 0)
    def _():
        carry_ref[...] = jnp.zeros(carry_ref.shape, F32)

    logit = _dot(a_ref[...], w_ref[...]) + bf_ref[...]
    log_f = -_softplus(-logit)
    cum = _prefix_sum_rows(log_f, tm) + carry_ref[...]
    carry_ref[...] = cum[tm - 1:tm, :]
    cum_ref[...] = cum


def _fox_gates(a, w_small, b_f, seq, tm=512):
    m, k = a.shape
    tm = min(tm, seq)
    return pl.pallas_call(
        functools.partial(_fox_gates_kernel, tm=tm, seq=seq),
        grid=(m // tm,),
        in_specs=[pl.BlockSpec((tm, k), lambda i: (i, 0)),
                  pl.BlockSpec((k, LANES), lambda i: (0, 0)),
                  pl.BlockSpec((1, LANES), lambda i: (0, 0))],
        out_specs=pl.BlockSpec((tm, LANES), lambda i: (i, 0)),
        out_shape=jax.ShapeDtypeStruct((m, LANES), F32),
        scratch_shapes=[pltpu.VMEM((1, LANES), F32)],
        compiler_params=_params("arbitrary"),
        name="fox_gates",
    )(a, w_small, b_f)


LOG2E = 1.4426950408889634


def _bias_lanes(c, sign_first):
    hi = c.astype(BF16)
    rem = c - hi.astype(F32)
    mid = rem.astype(BF16)
    lo = (rem - mid.astype(F32)).astype(BF16)
    lane = lax.broadcasted_iota(jnp.int32, c.shape, 1)
    base = 0 if sign_first else 3
    sgn = 1.0 if sign_first else -1.0
    pieces = jnp.where(lane == base, hi.astype(F32),
                       jnp.where(lane == base + 1, mid.astype(F32), lo.astype(F32))) * sgn
    in_pieces = (lane >= base) & (lane < base + 3)
    in_ones = (lane >= 3 - base) & (lane < 6 - base)
    return jnp.where(in_pieces, pieces, jnp.where(in_ones, 1.0, 0.0)).astype(BF16)


def _fox_kernel(q_ref, k_ref, v_ref, og_ref, cum_ref, qg_ref, kg_ref, o_ref, ka_ref, va_ref,
                *, tq, heads_per_step):
    hb = heads_per_step
    head0 = pl.program_id(1) * hb
    i = pl.program_id(2)
    head_lanes = [slice(hh * HEAD_DIM, (hh + 1) * HEAD_DIM) for hh in range(hb)]

    def kv_rows(j):
        return pl.ds(pl.multiple_of(j * tq, tq), tq)

    cum = cum_ref[...]
    qa = []
    for hh in range(hb):
        c = _pick_lane(cum, head0 + hh) * LOG2E
        kn = (_rms(k_ref[:, head_lanes[hh]]) * kg_ref[...]).astype(BF16)
        ka_ref[hh, kv_rows(i), :] = jnp.concatenate([kn, _bias_lanes(c, False)], axis=1)
        v = v_ref[:, head_lanes[hh]].astype(BF16)
        ones_col = (lax.broadcasted_iota(jnp.int32, v.shape, 1) == 0).astype(BF16)
        va_ref[hh, kv_rows(i), :] = jnp.concatenate([v, ones_col], axis=1)
        qn = (_rms(q_ref[:, head_lanes[hh]]) * qg_ref[...] * (HEAD_DIM ** -0.5 * LOG2E)).astype(BF16)
        qa.append(jnp.concatenate([qn, _bias_lanes(c, True)], axis=1))

    def block(j, s, m_prev, acc, diagonal):
        if diagonal:
            rows = lax.broadcasted_iota(jnp.int32, (tq, tq), 0)
            cols = lax.broadcasted_iota(jnp.int32, (tq, tq), 1)
            s = [jnp.where(rows >= cols, s[hh], -jnp.inf) for hh in range(hb)]
        m_new = [jnp.maximum(m_prev[hh], jnp.max(s[hh], axis=1, keepdims=True)) for hh in range(hb)]
        p = [jnp.exp2(s[hh] - m_new[hh]).astype(BF16) for hh in range(hb)]
        pv = [_dot(p[hh], va_ref[hh, kv_rows(j), :]) for hh in range(hb)]
        acc = [jnp.exp2(m_prev[hh] - m_new[hh]) * acc[hh] + pv[hh] for hh in range(hb)]
        return m_new, acc

    def scores(j):
        return [_dot_nt(qa[hh], ka_ref[hh, kv_rows(j), :]) for hh in range(hb)]

    def body(j, carry):
        m_prev, acc = carry
        return block(j, scores(j), m_prev, acc, False)

    init = ([jnp.full((tq, 1), -jnp.inf, F32) for _ in range(hb)],
            [jnp.zeros((tq, 2 * HEAD_DIM), F32) for _ in range(hb)])
    m_prev, acc = lax.fori_loop(0, i, body, init)
    _, acc = block(i, scores(i), m_prev, acc, True)
    for hh in range(hb):
        out = acc[hh][:, :HEAD_DIM] / acc[hh][:, HEAD_DIM:HEAD_DIM + 1]
        o_ref[:, head_lanes[hh]] = (out * _sigmoid(og_ref[:, head_lanes[hh]])).astype(o_ref.dtype)


def _fox_core(proj, cum, q_gain, k_gain, batch, seq, tq=512, heads_per_step=4):
    m = proj.shape[0]
    heads = (proj.shape[1] // HEAD_DIM) // 4
    tq = min(tq, seq)
    nq = seq // tq
    hb = heads_per_step
    width = hb * HEAD_DIM
    groups = heads // hb
    row = lambda b, h, i: b * nq + i
    tile = lambda part: pl.BlockSpec((tq, width), lambda b, h, i: (row(b, h, i), part * groups + h))
    gain = pl.BlockSpec((1, HEAD_DIM), lambda b, h, i: (0, 0))
    return pl.pallas_call(
        functools.partial(_fox_kernel, tq=tq, heads_per_step=hb),
        grid=(batch, groups, nq),
        in_specs=[tile(0), tile(1), tile(2), tile(3),
                  pl.BlockSpec((tq, LANES), lambda b, h, i: (row(b, h, i), 0)),
                  gain, gain],
        out_specs=pl.BlockSpec((tq, width), lambda b, h, i: (row(b, h, i), h)),
        out_shape=jax.ShapeDtypeStruct((m, heads * HEAD_DIM), BF16),
        scratch_shapes=[pltpu.VMEM((hb, seq, 2 * HEAD_DIM), BF16), pltpu.VMEM((hb, seq, 2 * HEAD_DIM), BF16)],
        compiler_params=_params("arbitrary", "arbitrary", "arbitrary"),
        name="fox_core",
    )(proj, proj, proj, proj, cum, q_gain.reshape(1, HEAD_DIM), k_gain.reshape(1, HEAD_DIM))


def _pad_cols(w, width):
    return jnp.zeros((w.shape[0], width), w.dtype).at[:, :w.shape[1]].set(w)


def _pad_lanes(v):
    return jnp.zeros((1, LANES), F32).at[0, :v.shape[0]].set(v.astype(F32))


def kernel(x, c, ada_w, ada_b, norm_mix_g, norm_ffn_g, gdn_w_in, gdn_conv_w, gdn_A_log, gdn_dt_bias, gdn_norm_g, gdn_w_out, fox_w_in, fox_b_f, fox_q_norm_g, fox_k_norm_g, fox_w_out, ffn_w_up, ffn_conv_w, ffn_w_down, final_norm_g):
    batch, seq, d = x.shape
    depth = ada_w.shape[0]
    heads = d // HEAD_DIM
    mix = heads * HEAD_DIM
    m = batch * seq

    mod = _ada_mod(c, ada_w, ada_b)
    xf = x.reshape(m, d)
    for layer in range(depth):
        sh1, sc1, g1, sh2, sc2, g2 = (mod[layer, :batch, p * d:(p + 1) * d].reshape(batch, 1, d)
                                      for p in range(6))
        j = layer // 2
        hx = _norm_mod(xf, norm_mix_g[layer], sc1, sh1, seq)
        if layer % 2 == 0:
            w_in = gdn_w_in.astype(BF16)
            proj = _matmul(hx, w_in, j, 4 * mix, tm=1024, tn=512)
            gate_w = w_in[j, :, 4 * mix:]
            w_small = jnp.concatenate([_pad_cols(gate_w[:, :heads], LANES),
                                       _pad_cols(gate_w[:, heads:], LANES)], axis=1)
            beta, g = _gdn_gates(hx, w_small, _pad_lanes(gdn_A_log[j]), _pad_lanes(gdn_dt_bias[j]))
            mixed = _gdn_core(proj, beta, g, gdn_conv_w[j], gdn_norm_g[j], batch, seq)
            w_out = gdn_w_out
        else:
            w_in = fox_w_in.astype(BF16)
            proj = _matmul(hx, w_in, j, 4 * mix, tm=1024, tn=512)
            gate_w = w_in[j, :, 4 * mix:]
            cum = _fox_gates(hx, _pad_cols(gate_w, LANES), _pad_lanes(fox_b_f[j]), seq)
            mixed = _fox_core(proj, cum, fox_q_norm_g[j], fox_k_norm_g[j], batch, seq)
            w_out = fox_w_out
        xf = _matmul_resid(mixed, w_out, j, xf, g1, seq, tm=1024, tn=512, weight_resident=True)
        hx = _norm_mod(xf, norm_ffn_g[layer], sc2, sh2, seq)
        act = _ffn_up(hx, ffn_w_up, ffn_conv_w, layer, seq)
        xf = _matmul_resid(act, ffn_w_down, layer, xf, g2, seq, tm=1024, tn=256, weight_resident=False)
    return _final_norm(xf, final_norm_g).reshape(batch, seq, d)
```

```python
import functools

import jax
import jax.numpy as jnp
from jax import lax
from jax.experimental import pallas as pl
from jax.experimental.pallas import tpu as pltpu

EPS = 1e-6
CHUNK = 64
HEAD_DIM = 128
LANES = 128
CARRY = 8
VMEM_LIMIT_BYTES = 56 * 1024 * 1024

F32 = jnp.float32
BF16 = jnp.bfloat16
HIGHEST = lax.Precision.HIGHEST


def _params(*sem):
    return pltpu.CompilerParams(dimension_semantics=sem, vmem_limit_bytes=VMEM_LIMIT_BYTES)


def _sigmoid(x):
    return 1.0 / (1.0 + jnp.exp(-x))


def _silu(x):
    return x * _sigmoid(x)


def _softplus(x):
    return jnp.maximum(x, 0.0) + jnp.log(1.0 + jnp.exp(-jnp.abs(x)))


def _rms(x):
    return x * lax.rsqrt(jnp.mean(x * x, axis=-1, keepdims=True) + EPS)


def _dot(a, b):
    return jnp.dot(a, b, preferred_element_type=F32)


def _dot_nt(a, b):
    return lax.dot_general(a, b, (((1,), (1,)), ((), ())), preferred_element_type=F32)


def _dot_tn(a, b):
    return lax.dot_general(a, b, (((0,), (0,)), ((), ())), preferred_element_type=F32)


def _bdot(a, b):
    return lax.dot_general(a.astype(BF16), b.astype(BF16), (((2,), (1,)), ((0,), (0,))),
                           preferred_element_type=F32)


def _bdot_nt(a, b):
    return lax.dot_general(a, b, (((2,), (2,)), ((0,), (0,))), preferred_element_type=F32)


def _pick_lane(tile, lane):
    only = jnp.where(lax.broadcasted_iota(jnp.int32, tile.shape, 1) == lane, tile, 0.0)
    return jnp.broadcast_to(jnp.sum(only, axis=1, keepdims=True), tile.shape)


def _prefix_sum_rows(x, span):
    row = lax.broadcasted_iota(jnp.int32, x.shape, 0) & (span - 1)
    s = 1
    while s < span:
        x = x + jnp.where(row >= s, pltpu.roll(x, s, 0), 0.0)
        s *= 2
    return x


def _ada_kernel(c_ref, w_ref, b_ref, o_ref):
    c = c_ref[...]
    o_ref[...] = _dot(_silu(c).astype(BF16), w_ref[...].astype(BF16)) + b_ref[...]


def _ada_mod(c, ada_w, ada_b, tn=1024):
    depth, d, n = ada_w.shape
    rows = 8
    c_pad = jnp.zeros((rows, d), F32).at[:c.shape[0]].set(c)
    return pl.pallas_call(
        _ada_kernel,
        grid=(depth, n // tn),
        in_specs=[pl.BlockSpec((rows, d), lambda l, j: (0, 0)),
                  pl.BlockSpec((None, d, tn), lambda l, j: (l, 0, j)),
                  pl.BlockSpec((None, 1, tn), lambda l, j: (l, 0, j))],
        out_specs=pl.BlockSpec((None, rows, tn), lambda l, j: (l, 0, j)),
        out_shape=jax.ShapeDtypeStruct((depth, rows, n), F32),
        compiler_params=_params("arbitrary", "arbitrary"),
        name="ada_mod",
    )(c_pad, ada_w, ada_b.reshape(depth, 1, n))


def _norm_mod_kernel(x_ref, g_ref, sc_ref, sh_ref, o_ref):
    y = _rms(x_ref[...]) * g_ref[...]
    o_ref[...] = (y * (1.0 + sc_ref[...]) + sh_ref[...]).astype(o_ref.dtype)


def _norm_mod(x, g, sc, sh, seq, tm=256):
    m, d = x.shape
    row_vec = pl.BlockSpec((None, 1, d), lambda i: ((i * tm) // seq, 0, 0))
    return pl.pallas_call(
        _norm_mod_kernel,
        grid=(m // tm,),
        in_specs=[pl.BlockSpec((tm, d), lambda i: (i, 0)),
                  pl.BlockSpec((1, d), lambda i: (0, 0)), row_vec, row_vec],
        out_specs=pl.BlockSpec((tm, d), lambda i: (i, 0)),
        out_shape=jax.ShapeDtypeStruct((m, d), BF16),
        compiler_params=_params("arbitrary"),
        name="norm_mod",
    )(x, g.reshape(1, d), sc, sh)


def _final_norm_kernel(x_ref, g_ref, o_ref):
    o_ref[...] = _rms(x_ref[...]) * g_ref[...]


def _final_norm(x, g, tm=256):
    m, d = x.shape
    return pl.pallas_call(
        _final_norm_kernel,
        grid=(m // tm,),
        in_specs=[pl.BlockSpec((tm, d), lambda i: (i, 0)),
                  pl.BlockSpec((1, d), lambda i: (0, 0))],
        out_specs=pl.BlockSpec((tm, d), lambda i: (i, 0)),
        out_shape=jax.ShapeDtypeStruct((m, d), F32),
        compiler_params=_params("arbitrary"),
        name="final_norm",
    )(x, g.reshape(1, d))


def _mm_nt_kernel(a_ref, wt_ref, o_ref):
    o_ref[...] = _dot_nt(a_ref[...], wt_ref[...].astype(BF16)).astype(o_ref.dtype)


def _matmul_nt(a, wt, n_out, tm, tn, out_dtype=F32):
    m, k = a.shape
    tm, tn = min(tm, m), min(tn, n_out)
    return pl.pallas_call(
        _mm_nt_kernel,
        grid=(n_out // tn, m // tm),
        in_specs=[pl.BlockSpec((tm, k), lambda j, i: (i, 0)),
                  pl.BlockSpec((tn, k), lambda j, i: (j, 0))],
        out_specs=pl.BlockSpec((tm, tn), lambda j, i: (i, j)),
        out_shape=jax.ShapeDtypeStruct((m, n_out), out_dtype),
        compiler_params=_params("arbitrary", "arbitrary"),
        name="matmul",
    )(a, wt)


def _mm_resid_kernel(a_ref, w_ref, x_ref, g_ref, o_ref):
    o_ref[...] = x_ref[...] + g_ref[...] * _dot(a_ref[...], w_ref[...].astype(BF16))


def _matmul_resid(a, w_stack, layer, x, gate, seq, tm, tn, weight_resident):
    m, k = a.shape
    n = w_stack.shape[2]
    tm, tn = min(tm, m), min(tn, n)
    if weight_resident:
        grid = (n // tn, m // tm)
        ij = lambda f: (lambda j, i: f(i, j))
        a_spec = pl.BlockSpec((tm, k), ij(lambda i, j: (i, 0)))
    else:
        grid = (m // tm, n // tn)
        ij = lambda f: f
        a_spec = pl.BlockSpec((tm, k), lambda i, j: (i, 0), pipeline_mode=pl.Buffered(1))
    return pl.pallas_call(
        _mm_resid_kernel,
        grid=grid,
        in_specs=[a_spec,
                  pl.BlockSpec((None, k, tn), ij(lambda i, j: (layer, 0, j))),
                  pl.BlockSpec((tm, tn), ij(lambda i, j: (i, j))),
                  pl.BlockSpec((None, 1, tn), ij(lambda i, j: ((i * tm) // seq, 0, j)))],
        out_specs=pl.BlockSpec((tm, tn), ij(lambda i, j: (i, j))),
        out_shape=jax.ShapeDtypeStruct((m, n), F32),
        compiler_params=_params("arbitrary", "arbitrary"),
        name="matmul_resid",
    )(a, w_stack, x, gate)


def _ffn_up_kernel(a_ref, wg_ref, wv_ref, cg_ref, cv_ref, o_ref, carry_ref, *, tm, seq):
    i = pl.program_id(1)

    @pl.when((i * tm) % seq == 0)
    def _():
        carry_ref[...] = jnp.zeros(carry_ref.shape, F32)

    a = a_ref[...]

    def conv_branch(w_ref, c_ref, idx):
        p = _dot(a, w_ref[...].astype(BF16))
        pe = jnp.concatenate([carry_ref[idx], p], axis=0)
        carry_ref[idx] = p[tm - CARRY:]
        c = c_ref[...]
        y = c[0:1] * pltpu.roll(pe, 2, 0) + c[1:2] * pltpu.roll(pe, 1, 0) + c[2:3] * pe
        return y[CARRY:]

    gate = conv_branch(wg_ref, cg_ref, 0)
    val = conv_branch(wv_ref, cv_ref, 1)
    o_ref[...] = (_silu(gate) * val).astype(o_ref.dtype)


def _ffn_up(a, w_stack, conv_stack, layer, seq, tm=1024, tn=256):
    m, k = a.shape
    f = w_stack.shape[2] // 2
    taps = conv_stack.shape[1]
    tm = min(tm, seq)
    nj = f // tn
    return pl.pallas_call(
        functools.partial(_ffn_up_kernel, tm=tm, seq=seq),
        grid=(nj, m // tm),
        in_specs=[pl.BlockSpec((tm, k), lambda j, i: (i, 0)),
                  pl.BlockSpec((None, k, tn), lambda j, i: (layer, 0, j)),
                  pl.BlockSpec((None, k, tn), lambda j, i: (layer, 0, j + nj)),
                  pl.BlockSpec((None, taps, tn), lambda j, i: (layer, 0, j)),
                  pl.BlockSpec((None, taps, tn), lambda j, i: (layer, 0, j + nj))],
        out_specs=pl.BlockSpec((tm, tn), lambda j, i: (i, j)),
        out_shape=jax.ShapeDtypeStruct((m, f), BF16),
        scratch_shapes=[pltpu.VMEM((2, CARRY, tn), F32)],
        compiler_params=_params("arbitrary", "arbitrary"),
        name="ffn_up",
    )(a, w_stack, w_stack, conv_stack, conv_stack)


def _gdn_gates_kernel(a_ref, wt_ref, alog_ref, dt_ref, beta_ref, g_ref):
    p = _dot_nt(a_ref[...], wt_ref[...].astype(BF16))
    beta_ref[...] = _sigmoid(p[:, :LANES])
    log_a = -jnp.exp(alog_ref[...]) * _softplus(p[:, LANES:] + dt_ref[...])
    g_ref[...] = _prefix_sum_rows(log_a, CHUNK)


def _gdn_gates(a, wt_small, a_log, dt_bias, tm=512):
    m, k = a.shape
    tm = min(tm, m)
    out = jax.ShapeDtypeStruct((m, LANES), F32)
    return pl.pallas_call(
        _gdn_gates_kernel,
        grid=(m // tm,),
        in_specs=[pl.BlockSpec((tm, k), lambda i: (i, 0)),
                  pl.BlockSpec((2 * LANES, k), lambda i: (0, 0)),
                  pl.BlockSpec((1, LANES), lambda i: (0, 0)),
                  pl.BlockSpec((1, LANES), lambda i: (0, 0))],
        out_specs=[pl.BlockSpec((tm, LANES), lambda i: (i, 0))] * 2,
        out_shape=[out, out],
        compiler_params=_params("arbitrary"),
        name="gdn_gates",
    )(a, wt_small, a_log, dt_bias)


def _gdn_kernel(q_ref, k_ref, v_ref, z_ref, beta_ref, g_ref, cq_ref, ck_ref, cv_ref, ng_ref,
                o_ref, state_ref, carry_ref, *, t_rows, heads_per_step):
    @pl.when(pl.program_id(2) == 0)
    def _():
        state_ref[...] = jnp.zeros(state_ref.shape, F32)
        carry_ref[...] = jnp.zeros(carry_ref.shape, F32)

    hb = heads_per_step
    ncl = t_rows // CHUNK
    nc = hb * ncl
    head_lanes = [slice(hh * HEAD_DIM, (hh + 1) * HEAD_DIM) for hh in range(hb)]

    def conv_silu(x_ref, c_ref, idx, hh):
        x = x_ref[:, head_lanes[hh]]
        xe = jnp.concatenate([carry_ref[idx, hh], x], axis=0)
        carry_ref[idx, hh] = x[t_rows - CARRY:]
        c = c_ref[:, head_lanes[hh]]
        y = (c[0:1] * pltpu.roll(xe, 3, 0) + c[1:2] * pltpu.roll(xe, 2, 0)
             + c[2:3] * pltpu.roll(xe, 1, 0) + c[3:4] * xe)
        return _silu(y[CARRY:])

    def l2n(x):
        return x * lax.rsqrt(jnp.sum(x * x, axis=-1, keepdims=True) + EPS)

    def chunks(per_head):
        return jnp.concatenate([per_head(hh) for hh in range(hb)], axis=0).reshape(nc, CHUNK, HEAD_DIM)

    head0 = pl.program_id(1) * hb
    q = chunks(lambda hh: l2n(conv_silu(q_ref, cq_ref, 0, hh)) * (HEAD_DIM ** -0.5))
    k = chunks(lambda hh: l2n(conv_silu(k_ref, ck_ref, 1, hh)))
    v = chunks(lambda hh: conv_silu(v_ref, cv_ref, 2, hh))
    beta = chunks(lambda hh: _pick_lane(beta_ref[...], head0 + hh))
    g = chunks(lambda hh: _pick_lane(g_ref[...], head0 + hh))

    eg = jnp.exp(g)
    kb = k * beta
    vb = v * beta
    kg = kb * eg
    qg = q * eg
    g_last = g[:, CHUNK - 1, :]
    kt = k * jnp.exp(g_last[:, None, :] - g)
    chunk_decay = jnp.exp(g_last)

    pick = (lax.broadcasted_iota(jnp.int32, (nc, CHUNK, LANES), 1)
            == lax.broadcasted_iota(jnp.int32, (nc, CHUNK, LANES), 2)).astype(BF16)
    g_hi = g.astype(BF16)
    g_rem = g - g_hi.astype(F32)
    g_mid = g_rem.astype(BF16)
    g_lo = (g_rem - g_mid.astype(F32)).astype(BF16)
    g_row = _bdot_nt(pick, g_hi) + _bdot_nt(pick, g_mid) + _bdot_nt(pick, g_lo)
    ii = lax.broadcasted_iota(jnp.int32, (CHUNK, CHUNK), 0)
    jj = lax.broadcasted_iota(jnp.int32, (CHUNK, CHUNK), 1)
    incl = (ii >= jj)[None]
    strict = (ii > jj)[None]
    decay = jnp.where(incl, jnp.exp(jnp.minimum(g[:, :, :CHUNK] - g_row, 0.0)), 0.0)

    kbf = k.astype(BF16)
    m_mat = jnp.where(strict, _bdot_nt(kb.astype(BF16), kbf) * decay, 0.0)
    attn = _bdot_nt(q.astype(BF16), kbf) * decay

    t_mat = jnp.broadcast_to((ii == jj).astype(F32)[None], m_mat.shape)
    s = 1
    while s < CHUNK:
        pair = ((ii & -(2 * s)) == (jj & -(2 * s))) & ((ii & s) != 0) & ((jj & s) == 0)
        t_mat = t_mat - _bdot(t_mat, _bdot(jnp.where(pair[None], m_mat, 0.0), t_mat))
        s *= 2
    u = _bdot(t_mat, vb)
    w = _bdot(t_mat, kg)

    ub, wb, ktb, attn_b = u.astype(BF16), w.astype(BF16), kt.astype(BF16), attn.astype(BF16)
    out_lhs = (qg - _bdot(attn_b, wb)).astype(BF16)
    out_add = _bdot(attn_b, ub)
    states = [state_ref[hh] for hh in range(hb)]
    outs = [[] for _ in range(hb)]
    for cl in range(ncl):
        for hh in range(hb):
            c = hh * ncl + cl
            sb = states[hh].astype(BF16)
            outs[hh].append(_dot(out_lhs[c], sb) + out_add[c])
            feedback = _dot_tn(ktb[c], wb[c]).astype(BF16)
            states[hh] = (states[hh] * chunk_decay[c:c + 1, :]
                          + (_dot_tn(ktb[c], ub[c]) - _dot(feedback, sb)))
    for hh in range(hb):
        state_ref[hh] = states[hh]
        o = jnp.concatenate(outs[hh], axis=0)
        o_ref[:, head_lanes[hh]] = (_rms(o) * ng_ref[...] * _silu(z_ref[:, head_lanes[hh]])).astype(o_ref.dtype)


def _gdn_core(proj, beta, g, conv_w, norm_g, batch, seq, t_rows=512, heads_per_step=4):
    m = proj.shape[0]
    heads = conv_w.shape[1] // (3 * HEAD_DIM)
    t_rows = min(t_rows, seq)
    nt = seq // t_rows
    hb = heads_per_step
    width = hb * HEAD_DIM
    groups = heads // hb
    row = lambda b, h, t: b * nt + t
    col = lambda part: pl.BlockSpec((t_rows, width), lambda b, h, t: (row(b, h, t), part * groups + h))
    cw = lambda part: pl.BlockSpec((conv_w.shape[0], width), lambda b, h, t: (0, part * groups + h))
    small = pl.BlockSpec((t_rows, LANES), lambda b, h, t: (row(b, h, t), 0))
    return pl.pallas_call(
        functools.partial(_gdn_kernel, t_rows=t_rows, heads_per_step=hb),
        grid=(batch, groups, nt),
        in_specs=[col(0), col(1), col(2), col(3), small, small, cw(0), cw(1), cw(2),
                  pl.BlockSpec((1, HEAD_DIM), lambda b, h, t: (0, 0))],
        out_specs=pl.BlockSpec((t_rows, width), lambda b, h, t: (row(b, h, t), h)),
        out_shape=jax.ShapeDtypeStruct((m, heads * HEAD_DIM), BF16),
        scratch_shapes=[pltpu.VMEM((hb, HEAD_DIM, HEAD_DIM), F32),
                        pltpu.VMEM((3, hb, CARRY, HEAD_DIM), F32)],
        compiler_params=_params("arbitrary", "arbitrary", "arbitrary"),
        name="gdn_core",
    )(proj, proj, proj, proj, beta, g, conv_w, conv_w, conv_w, norm_g.reshape(1, HEAD_DIM))


def _fox_gates_kernel(a_ref, wt_ref, bf_ref, cum_ref, carry_ref, *, tm, seq):
    i = pl.program_id(0)

    @pl.when((i * tm) % seq == 0)
    def _():
        carry_ref[...] = jnp.zeros(carry_ref.shape, F32)

    logit = _dot_nt(a_ref[...], wt_ref[...].astype(BF16)) + bf_ref[...]
    log_f = -_softplus(-logit)
    cum = _prefix_sum_rows(log_f, tm) + carry_ref[...]
    carry_ref[...] = cum[tm - 1:tm, :]
    cum_ref[...] = cum


def _fox_gates(a, wt_small, b_f, seq, tm=512):
    m, k = a.shape
    tm = min(tm, seq)
    return pl.pallas_call(
        functools.partial(_fox_gates_kernel, tm=tm, seq=seq),
        grid=(m // tm,),
        in_specs=[pl.BlockSpec((tm, k), lambda i: (i, 0)),
                  pl.BlockSpec((LANES, k), lambda i: (0, 0)),
                  pl.BlockSpec((1, LANES), lambda i: (0, 0))],
        out_specs=pl.BlockSpec((tm, LANES), lambda i: (i, 0)),
        out_shape=jax.ShapeDtypeStruct((m, LANES), F32),
        scratch_shapes=[pltpu.VMEM((1, LANES), F32)],
        compiler_params=_params("arbitrary"),
        name="fox_gates",
    )(a, wt_small, b_f)


LOG2E = 1.4426950408889634


def _bias_lanes(c, sign_first):
    hi = c.astype(BF16)
    rem = c - hi.astype(F32)
    mid = rem.astype(BF16)
    lo = (rem - mid.astype(F32)).astype(BF16)
    lane = lax.broadcasted_iota(jnp.int32, c.shape, 1)
    base = 0 if sign_first else 3
    sgn = 1.0 if sign_first else -1.0
    pieces = jnp.where(lane == base, hi.astype(F32),
                       jnp.where(lane == base + 1, mid.astype(F32), lo.astype(F32))) * sgn
    in_pieces = (lane >= base) & (lane < base + 3)
    in_ones = (lane >= 3 - base) & (lane < 6 - base)
    return jnp.where(in_pieces, pieces, jnp.where(in_ones, 1.0, 0.0)).astype(BF16)


def _fox_kernel(q_ref, k_ref, v_ref, og_ref, cum_ref, qg_ref, kg_ref, o_ref, ka_ref, va_ref,
                *, tq, heads_per_step):
    hb = heads_per_step
    head0 = pl.program_id(1) * hb
    i = pl.program_id(2)
    head_lanes = [slice(hh * HEAD_DIM, (hh + 1) * HEAD_DIM) for hh in range(hb)]

    def kv_rows(j):
        return pl.ds(pl.multiple_of(j * tq, tq), tq)

    cum = cum_ref[...]
    qa = []
    for hh in range(hb):
        c = _pick_lane(cum, head0 + hh) * LOG2E
        kn = (_rms(k_ref[:, head_lanes[hh]]) * kg_ref[...]).astype(BF16)
        ka_ref[hh, kv_rows(i), :] = jnp.concatenate([kn, _bias_lanes(c, False)], axis=1)
        v = v_ref[:, head_lanes[hh]].astype(BF16)
        ones_col = (lax.broadcasted_iota(jnp.int32, v.shape, 1) == 0).astype(BF16)
        va_ref[hh, kv_rows(i), :] = jnp.concatenate([v, ones_col], axis=1)
        qn = (_rms(q_ref[:, head_lanes[hh]]) * qg_ref[...] * (HEAD_DIM ** -0.5 * LOG2E)).astype(BF16)
        qa.append(jnp.concatenate([qn, _bias_lanes(c, True)], axis=1))

    def block(j, s, m_prev, acc, diagonal):
        if diagonal:
            rows = lax.broadcasted_iota(jnp.int32, (tq, tq), 0)
            cols = lax.broadcasted_iota(jnp.int32, (tq, tq), 1)
            s = [jnp.where(rows >= cols, s[hh], -jnp.inf) for hh in range(hb)]
        m_new = [jnp.maximum(m_prev[hh], jnp.max(s[hh], axis=1, keepdims=True)) for hh in range(hb)]
        p = [jnp.exp2(s[hh] - m_new[hh]).astype(BF16) for hh in range(hb)]
        pv = [_dot(p[hh], va_ref[hh, kv_rows(j), :]) for hh in range(hb)]
        acc = [jnp.exp2(m_prev[hh] - m_new[hh]) * acc[hh] + pv[hh] for hh in range(hb)]
        return m_new, acc

    def scores(j):
        return [_dot_nt(qa[hh], ka_ref[hh, kv_rows(j), :]) for hh in range(hb)]

    def body(j, carry):
        m_prev, acc = carry
        return block(j, scores(j), m_prev, acc, False)

    init = ([jnp.full((tq, 1), -jnp.inf, F32) for _ in range(hb)],
            [jnp.zeros((tq, 2 * HEAD_DIM), F32) for _ in range(hb)])
    m_prev, acc = lax.fori_loop(0, i, body, init)
    _, acc = block(i, scores(i), m_prev, acc, True)
    for hh in range(hb):
        out = acc[hh][:, :HEAD_DIM] / acc[hh][:, HEAD_DIM:HEAD_DIM + 1]
        o_ref[:, head_lanes[hh]] = (out * _sigmoid(og_ref[:, head_lanes[hh]])).astype(o_ref.dtype)


def _fox_core(proj, cum, q_gain, k_gain, batch, seq, tq=512, heads_per_step=4):
    m = proj.shape[0]
    heads = (proj.shape[1] // HEAD_DIM) // 4
    tq = min(tq, seq)
    nq = seq // tq
    hb = heads_per_step
    width = hb * HEAD_DIM
    groups = heads // hb
    row = lambda b, h, i: b * nq + i
    tile = lambda part: pl.BlockSpec((tq, width), lambda b, h, i: (row(b, h, i), part * groups + h))
    gain = pl.BlockSpec((1, HEAD_DIM), lambda b, h, i: (0, 0))
    return pl.pallas_call(
        functools.partial(_fox_kernel, tq=tq, heads_per_step=hb),
        grid=(batch, groups, nq),
        in_specs=[tile(0), tile(1), tile(2), tile(3),
                  pl.BlockSpec((tq, LANES), lambda b, h, i: (row(b, h, i), 0)),
                  gain, gain],
        out_specs=pl.BlockSpec((tq, width), lambda b, h, i: (row(b, h, i), h)),
        out_shape=jax.ShapeDtypeStruct((m, heads * HEAD_DIM), BF16),
        scratch_shapes=[pltpu.VMEM((hb, seq, 2 * HEAD_DIM), BF16), pltpu.VMEM((hb, seq, 2 * HEAD_DIM), BF16)],
        compiler_params=_params("arbitrary", "arbitrary", "arbitrary"),
        name="fox_core",
    )(proj, proj, proj, proj, cum, q_gain.reshape(1, HEAD_DIM), k_gain.reshape(1, HEAD_DIM))


def _pad_rows(w, rows):
    return jnp.zeros((rows, w.shape[1]), w.dtype).at[:w.shape[0]].set(w)


def _pad_lanes(v):
    return jnp.zeros((1, LANES), F32).at[0, :v.shape[0]].set(v.astype(F32))


def kernel(x, c, ada_w, ada_b, norm_mix_g, norm_ffn_g, gdn_w_in, gdn_conv_w, gdn_A_log, gdn_dt_bias, gdn_norm_g, gdn_w_out, fox_w_in, fox_b_f, fox_q_norm_g, fox_k_norm_g, fox_w_out, ffn_w_up, ffn_conv_w, ffn_w_down, final_norm_g):
    batch, seq, d = x.shape
    depth = ada_w.shape[0]
    heads = d // HEAD_DIM
    mix = heads * HEAD_DIM
    m = batch * seq

    mod = _ada_mod(c, ada_w, ada_b)
    xf = x.reshape(m, d)
    for layer in range(depth):
        sh1, sc1, g1, sh2, sc2, g2 = (mod[layer, :batch, p * d:(p + 1) * d].reshape(batch, 1, d)
                                      for p in range(6))
        j = layer // 2
        hx = _norm_mod(xf, norm_mix_g[layer], sc1, sh1, seq)
        if layer % 2 == 0:
            wt = gdn_w_in[j].T
            proj = _matmul_nt(hx, wt, 4 * mix, tm=1024, tn=512)
            gate_wt = wt[4 * mix:]
            wt_small = jnp.concatenate([_pad_rows(gate_wt[:heads], LANES),
                                        _pad_rows(gate_wt[heads:], LANES)], axis=0)
            beta, g = _gdn_gates(hx, wt_small, _pad_lanes(gdn_A_log[j]), _pad_lanes(gdn_dt_bias[j]))
            mixed = _gdn_core(proj, beta, g, gdn_conv_w[j], gdn_norm_g[j], batch, seq)
            w_out = gdn_w_out
        else:
            wt = fox_w_in[j].T
            proj = _matmul_nt(hx, wt, 4 * mix, tm=1024, tn=512)
            gate_wt = wt[4 * mix:]
            cum = _fox_gates(hx, _pad_rows(gate_wt, LANES), _pad_lanes(fox_b_f[j]), seq)
            mixed = _fox_core(proj, cum, fox_q_norm_g[j], fox_k_norm_g[j], batch, seq)
            w_out = fox_w_out
        xf = _matmul_resid(mixed, w_out, j, xf, g1, seq, tm=1024, tn=512, weight_resident=True)
        hx = _norm_mod(xf, norm_ffn_g[layer], sc2, sh2, seq)
        act = _ffn_up(hx, ffn_w_up, ffn_conv_w, layer, seq)
        xf = _matmul_resid(act, ffn_w_down, layer, xf, g2, seq, tm=1024, tn=256, weight_resident=False)
    return _final_norm(xf, final_norm_g).reshape(batch, seq, d)
```

```python
import functools

import jax
import jax.numpy as jnp
from jax import lax
from jax.experimental import pallas as pl
from jax.experimental.pallas import tpu as pltpu

EPS = 1e-6
CHUNK = 64
HEAD_DIM = 128
LANES = 128
CARRY = 8
VMEM_LIMIT_BYTES = 56 * 1024 * 1024

F32 = jnp.float32
BF16 = jnp.bfloat16
HIGHEST = lax.Precision.HIGHEST


def _params(*sem):
    return pltpu.CompilerParams(dimension_semantics=sem, vmem_limit_bytes=VMEM_LIMIT_BYTES)


def _sigmoid(x):
    return 1.0 / (1.0 + jnp.exp(-x))


def _silu(x):
    h = 0.5 * x
    return h + h * jnp.tanh(h)


def _softplus(x):
    return jnp.maximum(x, 0.0) + jnp.log(1.0 + jnp.exp(-jnp.abs(x)))


def _rms(x):
    return x * lax.rsqrt(jnp.mean(x * x, axis=-1, keepdims=True) + EPS)


def _dot(a, b):
    return jnp.dot(a, b, preferred_element_type=F32)


def _dot_nt(a, b):
    return lax.dot_general(a, b, (((1,), (1,)), ((), ())), preferred_element_type=F32)


def _dot_tn(a, b):
    return lax.dot_general(a, b, (((0,), (0,)), ((), ())), preferred_element_type=F32)


def _bdot(a, b):
    return lax.dot_general(a.astype(BF16), b.astype(BF16), (((2,), (1,)), ((0,), (0,))),
                           preferred_element_type=F32)


def _bdot_nt(a, b):
    return lax.dot_general(a, b, (((2,), (2,)), ((0,), (0,))), preferred_element_type=F32)


def _pick_lane(tile, lane):
    only = jnp.where(lax.broadcasted_iota(jnp.int32, tile.shape, 1) == lane, tile, 0.0)
    return jnp.broadcast_to(jnp.sum(only, axis=1, keepdims=True), tile.shape)


def _prefix_sum_rows(x, span):
    row = lax.broadcasted_iota(jnp.int32, x.shape, 0) & (span - 1)
    s = 1
    while s < span:
        x = x + jnp.where(row >= s, pltpu.roll(x, s, 0), 0.0)
        s *= 2
    return x


def _ada_kernel(c_ref, w_ref, b_ref, o_ref):
    c = c_ref[...]
    o_ref[...] = _dot(_silu(c).astype(BF16), w_ref[...].astype(BF16)) + b_ref[...]


def _ada_mod(c, ada_w, ada_b, tn=1024):
    depth, d, n = ada_w.shape
    rows = 8
    c_pad = jnp.zeros((rows, d), F32).at[:c.shape[0]].set(c)
    return pl.pallas_call(
        _ada_kernel,
        grid=(depth, n // tn),
        in_specs=[pl.BlockSpec((rows, d), lambda l, j: (0, 0)),
                  pl.BlockSpec((None, d, tn), lambda l, j: (l, 0, j)),
                  pl.BlockSpec((None, 1, tn), lambda l, j: (l, 0, j))],
        out_specs=pl.BlockSpec((None, rows, tn), lambda l, j: (l, 0, j)),
        out_shape=jax.ShapeDtypeStruct((depth, rows, n), F32),
        compiler_params=_params("arbitrary", "arbitrary"),
        name="ada_mod",
    )(c_pad, ada_w, ada_b.reshape(depth, 1, n))


def _norm_mod_kernel(x_ref, g_ref, sc_ref, sh_ref, o_ref):
    y = _rms(x_ref[...]) * g_ref[...]
    o_ref[...] = (y * (1.0 + sc_ref[...]) + sh_ref[...]).astype(o_ref.dtype)


def _norm_mod(x, g, sc, sh, seq, tm=256):
    m, d = x.shape
    row_vec = pl.BlockSpec((None, 1, d), lambda i: ((i * tm) // seq, 0, 0))
    return pl.pallas_call(
        _norm_mod_kernel,
        grid=(m // tm,),
        in_specs=[pl.BlockSpec((tm, d), lambda i: (i, 0)),
                  pl.BlockSpec((1, d), lambda i: (0, 0)), row_vec, row_vec],
        out_specs=pl.BlockSpec((tm, d), lambda i: (i, 0)),
        out_shape=jax.ShapeDtypeStruct((m, d), BF16),
        compiler_params=_params("arbitrary"),
        name="norm_mod",
    )(x, g.reshape(1, d), sc, sh)


def _final_norm_kernel(x_ref, g_ref, o_ref):
    o_ref[...] = _rms(x_ref[...]) * g_ref[...]


def _final_norm(x, g, tm=256):
    m, d = x.shape
    return pl.pallas_call(
        _final_norm_kernel,
        grid=(m // tm,),
        in_specs=[pl.BlockSpec((tm, d), lambda i: (i, 0)),
                  pl.BlockSpec((1, d), lambda i: (0, 0))],
        out_specs=pl.BlockSpec((tm, d), lambda i: (i, 0)),
        out_shape=jax.ShapeDtypeStruct((m, d), F32),
        compiler_params=_params("arbitrary"),
        name="final_norm",
    )(x, g.reshape(1, d))


def _mm_nt_kernel(a_ref, wt_ref, o_ref):
    o_ref[...] = _dot_nt(a_ref[...], wt_ref[...].astype(BF16)).astype(o_ref.dtype)


def _matmul_nt(a, wt, n_out, tm, tn, out_dtype=F32):
    m, k = a.shape
    tm, tn = min(tm, m), min(tn, n_out)
    return pl.pallas_call(
        _mm_nt_kernel,
        grid=(n_out // tn, m // tm),
        in_specs=[pl.BlockSpec((tm, k), lambda j, i: (i, 0)),
                  pl.BlockSpec((tn, k), lambda j, i: (j, 0))],
        out_specs=pl.BlockSpec((tm, tn), lambda j, i: (i, j)),
        out_shape=jax.ShapeDtypeStruct((m, n_out), out_dtype),
        compiler_params=_params("arbitrary", "arbitrary"),
        name="matmul",
    )(a, wt)


def _mm_resid_kernel(a_ref, w_ref, x_ref, g_ref, o_ref):
    o_ref[...] = x_ref[...] + g_ref[...] * _dot(a_ref[...], w_ref[...].astype(BF16))


def _matmul_resid(a, w_stack, layer, x, gate, seq, tm, tn, weight_resident):
    m, k = a.shape
    n = w_stack.shape[2]
    tm, tn = min(tm, m), min(tn, n)
    if weight_resident:
        grid = (n // tn, m // tm)
        ij = lambda f: (lambda j, i: f(i, j))
        a_spec = pl.BlockSpec((tm, k), ij(lambda i, j: (i, 0)))
    else:
        grid = (m // tm, n // tn)
        ij = lambda f: f
        a_spec = pl.BlockSpec((tm, k), lambda i, j: (i, 0), pipeline_mode=pl.Buffered(1))
    return pl.pallas_call(
        _mm_resid_kernel,
        grid=grid,
        in_specs=[a_spec,
                  pl.BlockSpec((None, k, tn), ij(lambda i, j: (layer, 0, j))),
                  pl.BlockSpec((tm, tn), ij(lambda i, j: (i, j))),
                  pl.BlockSpec((None, 1, tn), ij(lambda i, j: ((i * tm) // seq, 0, j)))],
        out_specs=pl.BlockSpec((tm, tn), ij(lambda i, j: (i, j))),
        out_shape=jax.ShapeDtypeStruct((m, n), F32),
        compiler_params=_params("arbitrary", "arbitrary"),
        name="matmul_resid",
    )(a, w_stack, x, gate)


def _ffn_up_kernel(a_ref, wg_ref, wv_ref, cg_ref, cv_ref, o_ref, carry_ref, *, tm, seq):
    i = pl.program_id(1)

    @pl.when((i * tm) % seq == 0)
    def _():
        carry_ref[...] = jnp.zeros(carry_ref.shape, F32)

    a = a_ref[...]

    def conv_branch(w_ref, c_ref, idx):
        p = _dot(a, w_ref[...].astype(BF16))
        pe = jnp.concatenate([carry_ref[idx], p], axis=0)
        carry_ref[idx] = p[tm - CARRY:]
        c = c_ref[...]
        y = c[0:1] * pltpu.roll(pe, 2, 0) + c[1:2] * pltpu.roll(pe, 1, 0) + c[2:3] * pe
        return y[CARRY:]

    gate = conv_branch(wg_ref, cg_ref, 0)
    val = conv_branch(wv_ref, cv_ref, 1)
    o_ref[...] = (_silu(gate) * val).astype(o_ref.dtype)


def _ffn_up(a, w_stack, conv_stack, layer, seq, tm=1024, tn=256):
    m, k = a.shape
    f = w_stack.shape[2] // 2
    taps = conv_stack.shape[1]
    tm = min(tm, seq)
    nj = f // tn
    return pl.pallas_call(
        functools.partial(_ffn_up_kernel, tm=tm, seq=seq),
        grid=(nj, m // tm),
        in_specs=[pl.BlockSpec((tm, k), lambda j, i: (i, 0)),
                  pl.BlockSpec((None, k, tn), lambda j, i: (layer, 0, j)),
                  pl.BlockSpec((None, k, tn), lambda j, i: (layer, 0, j + nj)),
                  pl.BlockSpec((None, taps, tn), lambda j, i: (layer, 0, j)),
                  pl.BlockSpec((None, taps, tn), lambda j, i: (layer, 0, j + nj))],
        out_specs=pl.BlockSpec((tm, tn), lambda j, i: (i, j)),
        out_shape=jax.ShapeDtypeStruct((m, f), BF16),
        scratch_shapes=[pltpu.VMEM((2, CARRY, tn), F32)],
        compiler_params=_params("arbitrary", "arbitrary"),
        name="ffn_up",
    )(a, w_stack, w_stack, conv_stack, conv_stack)


def _gdn_gates_kernel(a_ref, wt_ref, alog_ref, dt_ref, beta_ref, g_ref):
    p = _dot_nt(a_ref[...], wt_ref[...].astype(BF16))
    beta_ref[...] = _sigmoid(p[:, :LANES])
    log_a = -jnp.exp(alog_ref[...]) * _softplus(p[:, LANES:] + dt_ref[...])
    g_ref[...] = _prefix_sum_rows(log_a, CHUNK)


def _gdn_gates(a, wt_small, a_log, dt_bias, tm=512):
    m, k = a.shape
    tm = min(tm, m)
    out = jax.ShapeDtypeStruct((m, LANES), F32)
    return pl.pallas_call(
        _gdn_gates_kernel,
        grid=(m // tm,),
        in_specs=[pl.BlockSpec((tm, k), lambda i: (i, 0)),
                  pl.BlockSpec((2 * LANES, k), lambda i: (0, 0)),
                  pl.BlockSpec((1, LANES), lambda i: (0, 0)),
                  pl.BlockSpec((1, LANES), lambda i: (0, 0))],
        out_specs=[pl.BlockSpec((tm, LANES), lambda i: (i, 0))] * 2,
        out_shape=[out, out],
        compiler_params=_params("arbitrary"),
        name="gdn_gates",
    )(a, wt_small, a_log, dt_bias)


def _gdn_kernel(q_ref, k_ref, v_ref, z_ref, beta_ref, g_ref, cq_ref, ck_ref, cv_ref, ng_ref,
                o_ref, state_ref, carry_ref, *, t_rows, heads_per_step):
    @pl.when(pl.program_id(2) == 0)
    def _():
        state_ref[...] = jnp.zeros(state_ref.shape, F32)
        carry_ref[...] = jnp.zeros(carry_ref.shape, F32)

    hb = heads_per_step
    ncl = t_rows // CHUNK
    nc = hb * ncl
    head_lanes = [slice(hh * HEAD_DIM, (hh + 1) * HEAD_DIM) for hh in range(hb)]

    def conv_silu(x_ref, c_ref, idx, hh):
        x = x_ref[:, head_lanes[hh]]
        xe = jnp.concatenate([carry_ref[idx, hh], x], axis=0)
        carry_ref[idx, hh] = x[t_rows - CARRY:]
        c = c_ref[:, head_lanes[hh]]
        y = (c[0:1] * pltpu.roll(xe, 3, 0) + c[1:2] * pltpu.roll(xe, 2, 0)
             + c[2:3] * pltpu.roll(xe, 1, 0) + c[3:4] * xe)
        return _silu(y[CARRY:])

    def l2n(x):
        return x * lax.rsqrt(jnp.sum(x * x, axis=-1, keepdims=True) + EPS)

    def chunks(per_head):
        return jnp.concatenate([per_head(hh) for hh in range(hb)], axis=0).reshape(nc, CHUNK, HEAD_DIM)

    head0 = pl.program_id(1) * hb
    q = chunks(lambda hh: l2n(conv_silu(q_ref, cq_ref, 0, hh)) * (HEAD_DIM ** -0.5))
    k = chunks(lambda hh: l2n(conv_silu(k_ref, ck_ref, 1, hh)))
    v = chunks(lambda hh: conv_silu(v_ref, cv_ref, 2, hh))
    beta = chunks(lambda hh: _pick_lane(beta_ref[...], head0 + hh))
    g = chunks(lambda hh: _pick_lane(g_ref[...], head0 + hh))

    eg = jnp.exp(g)
    kb = k * beta
    vb = v * beta
    kg = kb * eg
    qg = q * eg
    g_last = g[:, CHUNK - 1, :]
    kt = k * jnp.exp(g_last[:, None, :] - g)
    chunk_decay = jnp.exp(g_last)

    pick = (lax.broadcasted_iota(jnp.int32, (nc, CHUNK, LANES), 1)
            == lax.broadcasted_iota(jnp.int32, (nc, CHUNK, LANES), 2)).astype(BF16)
    g_hi = g.astype(BF16)
    g_rem = g - g_hi.astype(F32)
    g_mid = g_rem.astype(BF16)
    g_lo = (g_rem - g_mid.astype(F32)).astype(BF16)
    g_row = _bdot_nt(pick, g_hi) + _bdot_nt(pick, g_mid) + _bdot_nt(pick, g_lo)
    ii = lax.broadcasted_iota(jnp.int32, (CHUNK, CHUNK), 0)
    jj = lax.broadcasted_iota(jnp.int32, (CHUNK, CHUNK), 1)
    incl = (ii >= jj)[None]
    strict = (ii > jj)[None]
    decay = jnp.where(incl, jnp.exp(jnp.minimum(g[:, :, :CHUNK] - g_row, 0.0)), 0.0)

    kbf = k.astype(BF16)
    m_mat = jnp.where(strict, _bdot_nt(kb.astype(BF16), kbf) * decay, 0.0)
    attn = _bdot_nt(q.astype(BF16), kbf) * decay

    t_mat = jnp.broadcast_to((ii == jj).astype(F32)[None], m_mat.shape)
    s = 1
    while s < CHUNK:
        pair = ((ii & -(2 * s)) == (jj & -(2 * s))) & ((ii & s) != 0) & ((jj & s) == 0)
        t_mat = t_mat - _bdot(t_mat, _bdot(jnp.where(pair[None], m_mat, 0.0), t_mat))
        s *= 2
    u = _bdot(t_mat, vb)
    w = _bdot(t_mat, kg)

    ub, wb, ktb, attn_b = u.astype(BF16), w.astype(BF16), kt.astype(BF16), attn.astype(BF16)
    out_lhs = (qg - _bdot(attn_b, wb)).astype(BF16)
    out_add = _bdot(attn_b, ub)
    states = [state_ref[hh] for hh in range(hb)]
    outs = [[] for _ in range(hb)]
    for cl in range(ncl):
        for hh in range(hb):
            c = hh * ncl + cl
            sb = states[hh].astype(BF16)
            outs[hh].append(_dot(out_lhs[c], sb) + out_add[c])
            feedback = _dot_tn(ktb[c], wb[c]).astype(BF16)
            states[hh] = (states[hh] * chunk_decay[c:c + 1, :]
                          + (_dot_tn(ktb[c], ub[c]) - _dot(feedback, sb)))
    for hh in range(hb):
        state_ref[hh] = states[hh]
        o = jnp.concatenate(outs[hh], axis=0)
        o_ref[:, head_lanes[hh]] = (_rms(o) * ng_ref[...] * _silu(z_ref[:, head_lanes[hh]])).astype(o_ref.dtype)


def _gdn_core(proj, beta, g, conv_w, norm_g, batch, seq, t_rows=512, heads_per_step=4):
    m = proj.shape[0]
    heads = conv_w.shape[1] // (3 * HEAD_DIM)
    t_rows = min(t_rows, seq)
    nt = seq // t_rows
    hb = heads_per_step
    width = hb * HEAD_DIM
    groups = heads // hb
    row = lambda b, h, t: b * nt + t
    col = lambda part: pl.BlockSpec((t_rows, width), lambda b, h, t: (row(b, h, t), part * groups + h))
    cw = lambda part: pl.BlockSpec((conv_w.shape[0], width), lambda b, h, t: (0, part * groups + h))
    small = pl.BlockSpec((t_rows, LANES), lambda b, h, t: (row(b, h, t), 0))
    return pl.pallas_call(
        functools.partial(_gdn_kernel, t_rows=t_rows, heads_per_step=hb),
        grid=(batch, groups, nt),
        in_specs=[col(0), col(1), col(2), col(3), small, small, cw(0), cw(1), cw(2),
                  pl.BlockSpec((1, HEAD_DIM), lambda b, h, t: (0, 0))],
        out_specs=pl.BlockSpec((t_rows, width), lambda b, h, t: (row(b, h, t), h)),
        out_shape=jax.ShapeDtypeStruct((m, heads * HEAD_DIM), BF16),
        scratch_shapes=[pltpu.VMEM((hb, HEAD_DIM, HEAD_DIM), F32),
                        pltpu.VMEM((3, hb, CARRY, HEAD_DIM), F32)],
        compiler_params=_params("arbitrary", "arbitrary", "arbitrary"),
        name="gdn_core",
    )(proj, proj, proj, proj, beta, g, conv_w, conv_w, conv_w, norm_g.reshape(1, HEAD_DIM))


def _fox_gates_kernel(a_ref, wt_ref, bf_ref, cum_ref, carry_ref, *, tm, seq):
    i = pl.program_id(0)

    @pl.when((i * tm) % seq == 0)
    def _():
        carry_ref[...] = jnp.zeros(carry_ref.shape, F32)

    logit = _dot_nt(a_ref[...], wt_ref[...].astype(BF16)) + bf_ref[...]
    log_f = -_softplus(-logit)
    cum = _prefix_sum_rows(log_f, tm) + carry_ref[...]
    carry_ref[...] = cum[tm - 1:tm, :]
    cum_ref[...] = cum


def _fox_gates(a, wt_small, b_f, seq, tm=512):
    m, k = a.shape
    tm = min(tm, seq)
    return pl.pallas_call(
        functools.partial(_fox_gates_kernel, tm=tm, seq=seq),
        grid=(m // tm,),
        in_specs=[pl.BlockSpec((tm, k), lambda i: (i, 0)),
                  pl.BlockSpec((LANES, k), lambda i: (0, 0)),
                  pl.BlockSpec((1, LANES), lambda i: (0, 0))],
        out_specs=pl.BlockSpec((tm, LANES), lambda i: (i, 0)),
        out_shape=jax.ShapeDtypeStruct((m, LANES), F32),
        scratch_shapes=[pltpu.VMEM((1, LANES), F32)],
        compiler_params=_params("arbitrary"),
        name="fox_gates",
    )(a, wt_small, b_f)


LOG2E = 1.4426950408889634


def _bias_tiles(c):
    hi = c.astype(BF16).astype(F32)
    rem = c - hi
    mid = rem.astype(BF16).astype(F32)
    lo = rem - mid
    lane = lax.broadcasted_iota(jnp.int32, c.shape, 1)
    ones_hi = jnp.where((lane >= 3) & (lane < 6), 1.0, 0.0)
    ones_lo = jnp.where(lane < 3, 1.0, 0.0)
    first = jnp.where(lane == 0, hi, jnp.where(lane == 1, mid, jnp.where(lane == 2, lo, ones_hi)))
    second = jnp.where(lane == 3, -hi, jnp.where(lane == 4, -mid, jnp.where(lane == 5, -lo, ones_lo)))
    return first.astype(BF16), second.astype(BF16)


def _fox_kernel(q_ref, k_ref, v_ref, og_ref, cum_ref, qg_ref, kg_ref, o_ref, ka_ref, va_ref,
                *, tq, heads_per_step):
    hb = heads_per_step
    head0 = pl.program_id(1) * hb
    i = pl.program_id(2)
    head_lanes = [slice(hh * HEAD_DIM, (hh + 1) * HEAD_DIM) for hh in range(hb)]

    def kv_rows(j):
        return pl.ds(pl.multiple_of(j * tq, tq), tq)

    cum = cum_ref[...]
    qa = []
    for hh in range(hb):
        bias_q, bias_k = _bias_tiles(_pick_lane(cum, head0 + hh) * LOG2E)
        kn = (_rms(k_ref[:, head_lanes[hh]]) * kg_ref[...]).astype(BF16)
        ka_ref[hh, kv_rows(i), :] = jnp.concatenate([kn, bias_k], axis=1)
        v = v_ref[:, head_lanes[hh]].astype(BF16)
        ones_col = (lax.broadcasted_iota(jnp.int32, v.shape, 1) == 0).astype(BF16)
        va_ref[hh, kv_rows(i), :] = jnp.concatenate([v, ones_col], axis=1)
        qn = (_rms(q_ref[:, head_lanes[hh]]) * qg_ref[...] * (HEAD_DIM ** -0.5 * LOG2E)).astype(BF16)
        qa.append(jnp.concatenate([qn, bias_q], axis=1))

    def block(j, s, m_prev, acc, diagonal):
        if diagonal:
            rows = lax.broadcasted_iota(jnp.int32, (tq, tq), 0)
            cols = lax.broadcasted_iota(jnp.int32, (tq, tq), 1)
            s = [jnp.where(rows >= cols, s[hh], -jnp.inf) for hh in range(hb)]
        m_new = [jnp.maximum(m_prev[hh], jnp.max(s[hh], axis=1, keepdims=True)) for hh in range(hb)]
        p = [jnp.exp2(s[hh] - m_new[hh]).astype(BF16) for hh in range(hb)]
        pv = [_dot(p[hh], va_ref[hh, kv_rows(j), :]) for hh in range(hb)]
        acc = [jnp.exp2(m_prev[hh] - m_new[hh]) * acc[hh] + pv[hh] for hh in range(hb)]
        return m_new, acc

    def scores(j):
        return [_dot_nt(qa[hh], ka_ref[hh, kv_rows(j), :]) for hh in range(hb)]

    def body(j, carry):
        m_prev, acc = carry
        return block(j, scores(j), m_prev, acc, False)

    init = ([jnp.full((tq, 1), -jnp.inf, F32) for _ in range(hb)],
            [jnp.zeros((tq, 2 * HEAD_DIM), F32) for _ in range(hb)])
    m_prev, acc = lax.fori_loop(0, i, body, init)
    _, acc = block(i, scores(i), m_prev, acc, True)
    for hh in range(hb):
        out = acc[hh][:, :HEAD_DIM] / acc[hh][:, HEAD_DIM:HEAD_DIM + 1]
        o_ref[:, head_lanes[hh]] = (out * _sigmoid(og_ref[:, head_lanes[hh]])).astype(o_ref.dtype)


def _fox_core(proj, cum, q_gain, k_gain, batch, seq, tq=512, heads_per_step=4):
    m = proj.shape[0]
    heads = (proj.shape[1] // HEAD_DIM) // 4
    tq = min(tq, seq)
    nq = seq // tq
    hb = heads_per_step
    width = hb * HEAD_DIM
    groups = heads // hb
    row = lambda b, h, i: b * nq + i
    tile = lambda part: pl.BlockSpec((tq, width), lambda b, h, i: (row(b, h, i), part * groups + h))
    gain = pl.BlockSpec((1, HEAD_DIM), lambda b, h, i: (0, 0))
    return pl.pallas_call(
        functools.partial(_fox_kernel, tq=tq, heads_per_step=hb),
        grid=(batch, groups, nq),
        in_specs=[tile(0), tile(1), tile(2), tile(3),
                  pl.BlockSpec((tq, LANES), lambda b, h, i: (row(b, h, i), 0)),
                  gain, gain],
        out_specs=pl.BlockSpec((tq, width), lambda b, h, i: (row(b, h, i), h)),
        out_shape=jax.ShapeDtypeStruct((m, heads * HEAD_DIM), BF16),
        scratch_shapes=[pltpu.VMEM((hb, seq, 2 * HEAD_DIM), BF16), pltpu.VMEM((hb, seq, 2 * HEAD_DIM), BF16)],
        compiler_params=_params("arbitrary", "arbitrary", "arbitrary"),
        name="fox_core",
    )(proj, proj, proj, proj, cum, q_gain.reshape(1, HEAD_DIM), k_gain.reshape(1, HEAD_DIM))


def _pad_rows(w, rows):
    return jnp.zeros((rows, w.shape[1]), w.dtype).at[:w.shape[0]].set(w)


def _pad_lanes(v):
    return jnp.zeros((1, LANES), F32).at[0, :v.shape[0]].set(v.astype(F32))


def kernel(x, c, ada_w, ada_b, norm_mix_g, norm_ffn_g, gdn_w_in, gdn_conv_w, gdn_A_log, gdn_dt_bias, gdn_norm_g, gdn_w_out, fox_w_in, fox_b_f, fox_q_norm_g, fox_k_norm_g, fox_w_out, ffn_w_up, ffn_conv_w, ffn_w_down, final_norm_g):
    batch, seq, d = x.shape
    depth = ada_w.shape[0]
    heads = d // HEAD_DIM
    mix = heads * HEAD_DIM
    m = batch * seq

    mod = _ada_mod(c, ada_w, ada_b)
    xf = x.reshape(m, d)
    for layer in range(depth):
        sh1, sc1, g1, sh2, sc2, g2 = (mod[layer, :batch, p * d:(p + 1) * d].reshape(batch, 1, d)
                                      for p in range(6))
        j = layer // 2
        hx = _norm_mod(xf, norm_mix_g[layer], sc1, sh1, seq)
        if layer % 2 == 0:
            wt = gdn_w_in[j].T
            proj = _matmul_nt(hx, wt, 4 * mix, tm=512, tn=1024)
            gate_wt = wt[4 * mix:]
            wt_small = jnp.concatenate([_pad_rows(gate_wt[:heads], LANES),
                                        _pad_rows(gate_wt[heads:], LANES)], axis=0)
            beta, g = _gdn_gates(hx, wt_small, _pad_lanes(gdn_A_log[j]), _pad_lanes(gdn_dt_bias[j]))
            mixed = _gdn_core(proj, beta, g, gdn_conv_w[j], gdn_norm_g[j], batch, seq)
            w_out = gdn_w_out
        else:
            wt = fox_w_in[j].T
            proj = _matmul_nt(hx, wt, 4 * mix, tm=512, tn=1024)
            gate_wt = wt[4 * mix:]
            cum = _fox_gates(hx, _pad_rows(gate_wt, LANES), _pad_lanes(fox_b_f[j]), seq)
            mixed = _fox_core(proj, cum, fox_q_norm_g[j], fox_k_norm_g[j], batch, seq)
            w_out = fox_w_out
        xf = _matmul_resid(mixed, w_out, j, xf, g1, seq, tm=512, tn=1024, weight_resident=True)
        hx = _norm_mod(xf, norm_ffn_g[layer], sc2, sh2, seq)
        act = _ffn_up(hx, ffn_w_up, ffn_conv_w, layer, seq)
        xf = _matmul_resid(act, ffn_w_down, layer, xf, g2, seq, tm=1024, tn=256, weight_resident=False)
    return _final_norm(xf, final_norm_g).reshape(batch, seq, d)
```
